```python
import math
import jax, jax.numpy as jnp
from jax import lax
import numpy as np

D_MODEL = 2048
BATCH = 4
SEQ = 8192
DEPTH = 1

N_META = 16
SSM_WIDTH = D_MODEL // 2
SSM_GROUP = 16
SSM_GROUPS = SSM_WIDTH // SSM_GROUP
SSM_STATE = 64
DT_MIN = 1e-3
DT_MAX = 1e-1
HEAD_DIM = 128
N_HEADS = (D_MODEL // 2) // HEAD_DIM
N_KV_HEADS = 2
ATTN_WIDTH = N_HEADS * HEAD_DIM
KV_WIDTH = N_KV_HEADS * HEAD_DIM
WINDOW = 128
BLOCK = 128
N_BRANCH = 2
IN_COLS = SSM_WIDTH + ATTN_WIDTH + 2 * KV_WIDTH + N_BRANCH * D_MODEL
PEER_HEADS = 8
PEER_KEYS = 128
PEER_N_EXPERTS = PEER_KEYS * PEER_KEYS
PEER_QDIM = 256
PEER_TOPK = 16
PEER_TOKEN_BLOCK = 256
NORM_EPS = 1e-6
NEG_INF = -1e30

kernel_name = "hybrid_s5_swa_peer_encoder"


def rms_norm(x, g):
    xf = x.astype(jnp.float32)
    y = xf * lax.rsqrt(jnp.mean(xf * xf, axis=-1, keepdims=True) + NORM_EPS)
    return (y * g.astype(jnp.float32)).astype(x.dtype)


def _s5_discretize(lam_re, lam_im, log_dt, b_re, b_im):
    f32 = jnp.float32
    dt = jnp.exp(log_dt.astype(f32))[:, None]
    lr = lam_re.astype(f32)
    li = lam_im.astype(f32)
    mag = jnp.exp(lr * dt)
    a_re = mag * jnp.cos(li * dt)
    a_im = mag * jnp.sin(li * dt)
    den = lr * lr + li * li
    nr = a_re - 1.0
    ni = a_im
    coef_re = (nr * lr + ni * li) / den
    coef_im = (ni * lr - nr * li) / den
    br = b_re.astype(f32)
    bi = b_im.astype(f32)
    bb_re = coef_re[..., None] * br - coef_im[..., None] * bi
    bb_im = coef_re[..., None] * bi + coef_im[..., None] * br
    return a_re, a_im, bb_re, bb_im


def _complex_linear_combine(e1, e2):
    a1r, a1i, b1r, b1i = e1
    a2r, a2i, b2r, b2i = e2
    ar = a2r * a1r - a2i * a1i
    ai = a2r * a1i + a2i * a1r
    br = a2r * b1r - a2i * b1i + b2r
    bi = a2r * b1i + a2i * b1r + b2i
    return ar, ai, br, bi


def _s5_direction(u_g, a_re, a_im, bb_re, bb_im, c_re, c_im, reverse):
    bu_re = jnp.einsum('lgc,gpc->lgp', u_g, bb_re)
    bu_im = jnp.einsum('lgc,gpc->lgp', u_g, bb_im)
    ar = jnp.broadcast_to(a_re, bu_re.shape)
    ai = jnp.broadcast_to(a_im, bu_re.shape)
    _, _, h_re, h_im = lax.associative_scan(
        _complex_linear_combine, (ar, ai, bu_re, bu_im), reverse=reverse, axis=0)
    cr = c_re.astype(jnp.float32)
    ci = c_im.astype(jnp.float32)
    return jnp.einsum('lgp,gcp->lgc', h_re, cr) - jnp.einsum('lgp,gcp->lgc', h_im, ci)


def s5_mixer(u, lam_re, lam_im, log_dt, b_re, b_im, c_re, c_im, d_skip, w_glu, b_glu):
    f32 = jnp.float32
    L = u.shape[1]
    disc = [_s5_discretize(lam_re[r], lam_im[r], log_dt[r], b_re[r], b_im[r]) for r in range(2)]

    def per_seq(u_seq):
        ug = u_seq.astype(f32).reshape(L, SSM_GROUPS, SSM_GROUP)
        y_fwd = _s5_direction(ug, *disc[0], c_re[0], c_im[0], False)
        y_bwd = _s5_direction(ug, *disc[1], c_re[1], c_im[1], True)
        return (y_fwd + y_bwd).reshape(L, SSM_WIDTH)

    uf = u.astype(f32)
    y = lax.map(per_seq, uf) + d_skip.astype(f32) * uf
    y = jax.nn.gelu(y)
    y = y * jax.nn.sigmoid(y @ w_glu.astype(f32) + b_glu.astype(f32))
    return y.astype(u.dtype)


def _to_layout(t):
    pad = jnp.zeros((t.shape[0], BLOCK - N_META) + t.shape[2:], t.dtype)
    return jnp.concatenate([t[:, :N_META], pad, t[:, N_META:]], axis=1)


def _from_layout(t):
    return jnp.concatenate([t[:, :N_META], t[:, BLOCK:]], axis=1)


def _band(t, nb):
    tb = t.reshape((t.shape[0], nb, BLOCK) + t.shape[2:])
    widths = [(0, 0), (1, 1)] + [(0, 0)] * (tb.ndim - 2)
    tb = jnp.pad(tb, widths)
    return jnp.concatenate([tb[:, :-2], tb[:, 1:-1], tb[:, 2:]], axis=2)


def _alibi_slopes():
    s = 2.0 ** (-8.0 * np.arange(1, N_HEADS + 1) / N_HEADS)
    return jnp.asarray(s, dtype=jnp.float32)


def windowed_gqa(q, k, v, sink):
    f32 = jnp.float32
    bsz = q.shape[0]
    grp = N_HEADS // N_KV_HEADS
    qp = _to_layout(q * (HEAD_DIM ** -0.5))
    kp = _to_layout(k)
    vp = _to_layout(v)
    Lp = qp.shape[1]
    nb = Lp // BLOCK
    idx = jnp.arange(Lp, dtype=jnp.int32)
    is_meta = idx < N_META
    is_real = idx >= BLOCK
    valid = is_meta | is_real
    pos = jnp.where(is_real, idx - BLOCK + N_META, idx)

    qb = qp.reshape(bsz, nb, BLOCK, N_KV_HEADS, grp, HEAD_DIM)
    kb = _band(kp, nb)
    vb = _band(vp, nb)
    pos_q = pos.reshape(nb, BLOCK)
    pos_b = _band(pos[None], nb)[0]
    valid_b = _band(valid[None], nb)[0]
    meta_b = _band(is_meta[None], nb)[0]
    dist = jnp.abs(pos_q[:, :, None] - pos_b[:, None, :]).astype(f32)
    allowed = valid_b[:, None, :] & (~meta_b[:, None, :]) & (dist <= WINDOW)
    slopes = _alibi_slopes().reshape(N_KV_HEADS, grp)
    bias = -slopes[None, :, :, None, None] * dist[:, None, None]

    s_band = jnp.einsum('bnqkgd,bnskd->bnkgqs', qb, kb, preferred_element_type=f32)
    s_band = jnp.where(allowed[:, None, None], s_band + bias, NEG_INF)
    k_meta = k[:, :N_META]
    v_meta = v[:, :N_META]
    s_meta = jnp.einsum('bnqkgd,bmkd->bnkgqm', qb, k_meta, preferred_element_type=f32)
    sink_l = jnp.broadcast_to(sink.astype(f32).reshape(N_KV_HEADS, grp)[None, None, :, :, None, None],
                              s_meta.shape[:-1] + (1,))
    p = jax.nn.softmax(jnp.concatenate([s_band, s_meta, sink_l], axis=-1), axis=-1)
    p_band = p[..., :3 * BLOCK].astype(v.dtype)
    p_meta = p[..., 3 * BLOCK:3 * BLOCK + N_META].astype(v.dtype)
    out = (jnp.einsum('bnkgqs,bnskd->bnqkgd', p_band, vb)
           + jnp.einsum('bnkgqm,bmkd->bnqkgd', p_meta, v_meta))
    out = out.reshape(bsz, Lp, ATTN_WIDTH)
    return _from_layout(out)


def peer_ffn(h, w_q, sub_k1, sub_k2, u_emb, v_emb):
    f32 = jnp.float32
    bsz, L, D = h.shape
    ht = h.reshape(-1, D)
    T = ht.shape[0]
    half = PEER_QDIM // 2
    q = (ht @ w_q).reshape(T, PEER_HEADS, 2, half)
    s1 = jnp.einsum('thd,hnd->thn', q[:, :, 0], sub_k1, preferred_element_type=f32)
    s2 = jnp.einsum('thd,hnd->thn', q[:, :, 1], sub_k2, preferred_element_type=f32)
    v1, i1 = lax.top_k(s1, PEER_TOPK)
    v2, i2 = lax.top_k(s2, PEER_TOPK)
    cand = (v1[..., :, None] + v2[..., None, :]).reshape(T, PEER_HEADS, PEER_TOPK * PEER_TOPK)
    sc, ci = lax.top_k(cand, PEER_TOPK)
    e1 = jnp.take_along_axis(i1, ci // PEER_TOPK, axis=-1)
    e2 = jnp.take_along_axis(i2, ci % PEER_TOPK, axis=-1)
    eidx = e1 * PEER_KEYS + e2
    gate = jax.nn.softmax(sc, axis=-1)

    pad = (-T) % PEER_TOKEN_BLOCK
    nc = (T + pad) // PEER_TOKEN_BLOCK
    ht_p = jnp.pad(ht, ((0, pad), (0, 0))).reshape(nc, PEER_TOKEN_BLOCK, D)
    e_p = jnp.pad(eidx, ((0, pad), (0, 0), (0, 0))).reshape(nc, PEER_TOKEN_BLOCK, PEER_HEADS, PEER_TOPK)
    g_p = jnp.pad(gate, ((0, pad), (0, 0), (0, 0))).reshape(nc, PEER_TOKEN_BLOCK, PEER_HEADS, PEER_TOPK)

    def block(args):
        hc, ec, gc = args
        u = jnp.take(u_emb, ec, axis=0)
        a = jnp.einsum('cd,chkd->chk', hc, u, preferred_element_type=f32)
        w = (gc * jax.nn.gelu(a)).astype(hc.dtype)
        vv = jnp.take(v_emb, ec, axis=0)
        return jnp.einsum('chk,chkd->cd', w, vv).astype(hc.dtype)

    out = lax.map(block, (ht_p, e_p, g_p))
    return out.reshape(-1, D)[:T].reshape(bsz, L, D)


def setup_inputs(seed: int = 0) -> dict:
    key = jax.random.key(seed)
    ks = jax.random.split(key, 28)
    f32 = jnp.float32

    def nrm(k, shape, scale):
        return jax.random.normal(k, shape, f32) * scale

    G, P, C = SSM_GROUPS, SSM_STATE, SSM_GROUP
    n = jnp.arange(P, dtype=f32)
    return {
        "x": nrm(ks[0], (BATCH, SEQ, D_MODEL), 1.0),
        "meta_tokens": nrm(ks[1], (N_META, D_MODEL), 1.0),
        "norm_mix": 1.0 + nrm(ks[2], (DEPTH, D_MODEL), 0.05),
        "w_in": nrm(ks[3], (DEPTH, D_MODEL, IN_COLS), D_MODEL ** -0.5),
        "b_gate": nrm(ks[4], (DEPTH, N_BRANCH * D_MODEL), 0.02),
        "ssm_lam_re": -0.5 + nrm(ks[5], (DEPTH, 2, G, P), 0.01),
        "ssm_lam_im": math.pi * n + nrm(ks[6], (DEPTH, 2, G, P), 0.01),
        "ssm_log_dt": jax.random.uniform(ks[7], (DEPTH, 2, G), f32, math.log(DT_MIN), math.log(DT_MAX)),
        "ssm_b_re": nrm(ks[8], (DEPTH, 2, G, P, C), (2.0 * C) ** -0.5),
        "ssm_b_im": nrm(ks[9], (DEPTH, 2, G, P, C), (2.0 * C) ** -0.5),
        "ssm_c_re": nrm(ks[10], (DEPTH, 2, G, C, P), 0.5),
        "ssm_c_im": nrm(ks[11], (DEPTH, 2, G, C, P), 0.5),
        "ssm_d": nrm(ks[12], (DEPTH, SSM_WIDTH), 1.0),
        "ssm_w_glu": nrm(ks[13], (DEPTH, SSM_WIDTH, SSM_WIDTH), SSM_WIDTH ** -0.5),
        "ssm_b_glu": nrm(ks[14], (DEPTH, SSM_WIDTH), 0.02),
        "attn_sink": nrm(ks[15], (DEPTH, N_HEADS), 1.0),
        "w_proj_ssm": nrm(ks[16], (DEPTH, SSM_WIDTH, D_MODEL), SSM_WIDTH ** -0.5),
        "w_proj_attn": nrm(ks[17], (DEPTH, ATTN_WIDTH, D_MODEL), ATTN_WIDTH ** -0.5),
        "w_out": nrm(ks[18], (DEPTH, D_MODEL, D_MODEL), D_MODEL ** -0.5),
        "norm_ffn": 1.0 + nrm(ks[19], (DEPTH, D_MODEL), 0.05),
        "peer_w_q": nrm(ks[20], (DEPTH, D_MODEL, PEER_HEADS * PEER_QDIM), D_MODEL ** -0.5),
        "peer_k1": nrm(ks[21], (DEPTH, PEER_HEADS, PEER_KEYS, PEER_QDIM // 2), (PEER_QDIM // 2) ** -0.5),
        "peer_k2": nrm(ks[22], (DEPTH, PEER_HEADS, PEER_KEYS, PEER_QDIM // 2), (PEER_QDIM // 2) ** -0.5),
        "peer_u": nrm(ks[23], (DEPTH, PEER_N_EXPERTS, D_MODEL), D_MODEL ** -0.5),
        "peer_v": nrm(ks[24], (DEPTH, PEER_N_EXPERTS, D_MODEL), 0.5),
        "norm_final": 1.0 + nrm(ks[25], (D_MODEL,), 0.05),
    }


def reference(x, meta_tokens, norm_mix, w_in, b_gate, ssm_lam_re, ssm_lam_im, ssm_log_dt,
              ssm_b_re, ssm_b_im, ssm_c_re, ssm_c_im, ssm_d, ssm_w_glu, ssm_b_glu, attn_sink,
              w_proj_ssm, w_proj_attn, w_out, norm_ffn, peer_w_q, peer_k1, peer_k2, peer_u,
              peer_v, norm_final):
    bsz = x.shape[0]
    meta = jnp.broadcast_to(meta_tokens.astype(x.dtype)[None], (bsz, N_META, D_MODEL))
    hs = jnp.concatenate([meta, x], axis=1)
    L = hs.shape[1]
    splits = [SSM_WIDTH, SSM_WIDTH + ATTN_WIDTH, SSM_WIDTH + ATTN_WIDTH + KV_WIDTH,
              SSM_WIDTH + ATTN_WIDTH + 2 * KV_WIDTH]
    for l in range(DEPTH):
        h = rms_norm(hs, norm_mix[l])
        z = h @ w_in[l]
        u_ssm, q, k, v, gates = jnp.split(z, splits, axis=-1)
        y_a = s5_mixer(u_ssm, ssm_lam_re[l], ssm_lam_im[l], ssm_log_dt[l], ssm_b_re[l], ssm_b_im[l],
                       ssm_c_re[l], ssm_c_im[l], ssm_d[l], ssm_w_glu[l], ssm_b_glu[l])
        y_b = windowed_gqa(q.reshape(bsz, L, N_HEADS, HEAD_DIM),
                           k.reshape(bsz, L, N_KV_HEADS, HEAD_DIM),
                           v.reshape(bsz, L, N_KV_HEADS, HEAD_DIM), attn_sink[l])
        g = jax.nn.sigmoid((gates + b_gate[l]).astype(jnp.float32)).astype(hs.dtype)
        g = g.reshape(bsz, L, N_BRANCH, D_MODEL)
        merged = g[:, :, 0] * (y_a @ w_proj_ssm[l]) + g[:, :, 1] * (y_b @ w_proj_attn[l])
        hs = hs + merged @ w_out[l]
        hs = hs + peer_ffn(rms_norm(hs, norm_ffn[l]), peer_w_q[l], peer_k1[l], peer_k2[l],
                           peer_u[l], peer_v[l])
    hs = rms_norm(hs, norm_final)
    return hs[:, N_META:]
```

```python
import functools
import math

import numpy as np
import jax
import jax.numpy as jnp
from jax import lax
from jax.experimental import pallas as pl
from jax.experimental.pallas import tpu as pltpu

f32 = jnp.float32
bf16 = jnp.bfloat16

D_MODEL = 2048
N_META = 16
SSM_WIDTH = 1024
SSM_GROUP = 16
SSM_GROUPS = 64
SSM_STATE = 64
HEAD_DIM = 128
N_HEADS = 8
N_KV_HEADS = 2
ATTN_WIDTH = 1024
KV_WIDTH = 256
WINDOW = 128
BLOCK = 128
PEER_HEADS = 8
PEER_KEYS = 128
PEER_HALF = 128
PEER_TOPK = 16
PEER_SLOTS = PEER_HEADS * PEER_TOPK
NORM_EPS = 1e-6
NEG_INF = -1e30

PAD_FRONT = 256
VMEM_LIMIT = 56 * 1024 * 1024

HI = lax.Precision.HIGHEST


def _cparams(sem, vmem=None):
    return pltpu.CompilerParams(dimension_semantics=sem, vmem_limit_bytes=vmem)


def _rms(x, g):
    ms = jnp.mean(x * x, axis=-1, keepdims=True)
    return x * lax.rsqrt(ms + NORM_EPS) * g


def _dot_nt(a, b):
    return lax.dot_general(a, b, (((1,), (1,)), ((), ())), preferred_element_type=f32)


IN_TM = 512
IN_TN = 512
_IN_QJ = ATTN_WIDTH // IN_TN
_IN_KVJ = _IN_QJ + (2 * KV_WIDTH) // IN_TN


def _inproj_kernel(x_ref, g_ref, wut_ref, w2_ref, b2_ref, ut_ref, q_ref, kv_ref, gt_ref, xn_ref):
    j = pl.program_id(1)

    @pl.when(j == 0)
    def _():
        xn = _rms(x_ref[...], g_ref[...]).astype(bf16)
        xn_ref[...] = xn
        ut_ref[...] = _dot_nt(wut_ref[...], xn).astype(bf16)

    acc = jnp.dot(xn_ref[...], w2_ref[...], preferred_element_type=f32)

    @pl.when(j < _IN_QJ)
    def _():
        q_ref[...] = (acc * (HEAD_DIM ** -0.5)).astype(bf16)

    @pl.when((j >= _IN_QJ) & (j < _IN_KVJ))
    def _():
        kv_ref[...] = acc.astype(bf16)

    @pl.when(j >= _IN_KVJ)
    def _():
        gt_ref[...] = jax.nn.sigmoid(acc + b2_ref[...])


def _inproj(hs_pad, norm_g, w_in, b_gate):
    rows = hs_pad.shape[0]
    wut = w_in[:, :SSM_WIDTH].T.astype(bf16)
    w2 = w_in[:, SSM_WIDTH:].astype(bf16)
    n2 = w2.shape[1]
    b2 = jnp.concatenate([jnp.zeros((n2 - b_gate.shape[0],), f32), b_gate.astype(f32)])[None]
    nj = n2 // IN_TN
    ngate = 2 * D_MODEL
    return pl.pallas_call(
        _inproj_kernel,
        out_shape=(
            jax.ShapeDtypeStruct((SSM_WIDTH, rows), bf16),
            jax.ShapeDtypeStruct((rows, ATTN_WIDTH), bf16),
            jax.ShapeDtypeStruct((rows, 2 * KV_WIDTH), bf16),
            jax.ShapeDtypeStruct((rows, ngate), f32),
        ),
        grid=(rows // IN_TM, nj),
        in_specs=[
            pl.BlockSpec((IN_TM, D_MODEL), lambda i, j: (i, 0)),
            pl.BlockSpec((1, D_MODEL), lambda i, j: (0, 0)),
            pl.BlockSpec((SSM_WIDTH, D_MODEL), lambda i, j: (0, 0)),
            pl.BlockSpec((D_MODEL, IN_TN), lambda i, j: (0, j)),
            pl.BlockSpec((1, IN_TN), lambda i, j: (0, j)),
        ],
        out_specs=(
            pl.BlockSpec((SSM_WIDTH, IN_TM), lambda i, j: (0, i)),
            pl.BlockSpec((IN_TM, IN_TN), lambda i, j: (i, jnp.minimum(j, _IN_QJ - 1))),
            pl.BlockSpec((IN_TM, IN_TN), lambda i, j: (i, 0)),
            pl.BlockSpec((IN_TM, IN_TN), lambda i, j: (i, jnp.maximum(j - _IN_KVJ, 0))),
        ),
        scratch_shapes=[pltpu.VMEM((IN_TM, D_MODEL), bf16)],
        compiler_params=_cparams(("parallel", "arbitrary"), VMEM_LIMIT),
        name="inproj",
    )(hs_pad, norm_g.astype(f32)[None], wut, w2, b2)


Q = BLOCK
GW = SSM_GROUP * Q


def _s5_weights(lam_re, lam_im, log_dt, b_re, b_im, c_re, c_im):
    dt = jnp.exp(log_dt.astype(f32))[..., None]
    lr = lam_re.astype(f32)
    li = lam_im.astype(f32)
    k = jnp.arange(Q + 1, dtype=f32)
    mag = jnp.exp(lr[..., None] * dt[..., None] * k)
    ang = li[..., None] * dt[..., None] * k
    ar = mag * jnp.cos(ang)
    ai = mag * jnp.sin(ang)
    a1r, a1i = ar[..., 1], ai[..., 1]
    den = lr * lr + li * li
    nr, ni = a1r - 1.0, a1i
    cr = (nr * lr + ni * li) / den
    ci = (ni * lr - nr * li) / den
    br, bi = b_re.astype(f32), b_im.astype(f32)
    bbr = cr[..., None] * br - ci[..., None] * bi
    bbi = cr[..., None] * bi + ci[..., None] * br
    ccr = jnp.swapaxes(c_re.astype(f32), -1, -2)
    cci = jnp.swapaxes(c_im.astype(f32), -1, -2)

    cbr = ccr[..., :, None] * bbr[..., None, :] - cci[..., :, None] * bbi[..., None, :]
    cbi = ccr[..., :, None] * bbi[..., None, :] + cci[..., :, None] * bbr[..., None, :]
    G, P, C = cbr.shape[1], cbr.shape[2], cbr.shape[3]
    cbr2 = cbr.reshape(2, G, P, C * C)
    cbi2 = cbi.reshape(2, G, P, C * C)
    kk = (jnp.einsum('dgpk,dgpx->dgkx', ar[..., :Q], cbr2, precision=HI)
          - jnp.einsum('dgpk,dgpx->dgkx', ai[..., :Q], cbi2, precision=HI))
    kk = kk.reshape(2, G, Q, C, C)
    kf, kb = kk[0], kk[1]
    lag0 = (kf[:, 0] + kb[:, 0])[:, None]
    zero = jnp.zeros_like(lag0)
    kfull = jnp.concatenate([lag0, kf[:, 1:], zero, kb[:, 1:][:, ::-1]], axis=1)
    kfull = jnp.transpose(kfull, (0, 3, 2, 1))

    def bu_pow(d, pw_r, pw_i):
        re = pw_r[:, :, None, :] * bbr[d][..., None] - pw_i[:, :, None, :] * bbi[d][..., None]
        im = pw_r[:, :, None, :] * bbi[d][..., None] + pw_i[:, :, None, :] * bbr[d][..., None]
        return jnp.transpose(re, (0, 2, 3, 1)), jnp.transpose(im, (0, 2, 3, 1))
    fr, fi = bu_pow(0, ar[0][..., :Q][..., ::-1], ai[0][..., :Q][..., ::-1])
    rr, ri = bu_pow(1, ar[1][..., :Q], ai[1][..., :Q])
    wb = jnp.concatenate([fr, rr, fi, ri], axis=-1).reshape(G, C * Q, 4 * P)

    def c_pow(d, pw_r, pw_i):
        re = ccr[d][..., None] * pw_r[:, :, None, :] - cci[d][..., None] * pw_i[:, :, None, :]
        im = ccr[d][..., None] * pw_i[:, :, None, :] + cci[d][..., None] * pw_r[:, :, None, :]
        return re, -im
    f_re, f_im = c_pow(0, ar[0][..., 1:], ai[0][..., 1:])
    b_re_, b_im_ = c_pow(1, ar[1][..., 1:][..., ::-1], ai[1][..., 1:][..., ::-1])
    wc = jnp.concatenate([f_re, b_re_, f_im, b_im_], axis=1).reshape(G, 4 * P, C * Q)

    aqr, aqi = ar[..., Q], ai[..., Q]
    coef = jnp.stack([jnp.concatenate([aqr[0], aqr[1]], axis=-1),
                      jnp.concatenate([aqi[0], aqi[1]], axis=-1)], axis=1)
    coef = jnp.concatenate([coef, jnp.zeros((G, 6, 2 * P), f32)], axis=1)
    return kfull, wb.astype(bf16), wc.astype(bf16), coef


def _s5_kernel(nb, nc, ut_ref, kf_ref, wb_ref, wc_ref, coef_ref, dsk_ref, o_ref,
               x_ref, t_ref, cst_ref, hin_ref):
    C = SSM_GROUP
    P2 = 2 * SSM_STATE
    for c in range(C):
        x_ref[:, c * Q:(c + 1) * Q] = ut_ref[c]
    x = x_ref[...]

    def build(cp, carry):
        r0 = pl.multiple_of(cp * Q, Q)
        for c in range(C):
            row = kf_ref[0, cp, pl.ds(c, 1), :]
            blk = jnp.broadcast_to(row, (Q, 2 * Q))
            rolled = pltpu.roll(blk, 0, 1, stride=1, stride_axis=0)
            t_ref[pl.ds(r0, Q), c * Q:(c + 1) * Q] = rolled[:, :Q].astype(bf16)
        return carry
    lax.fori_loop(0, C, build, 0)

    cst_ref[...] = jnp.dot(x, wb_ref[0], preferred_element_type=f32)
    P = SSM_STATE
    aqr = coef_ref[0, 0:1, :]
    aqi = coef_ref[0, 1:2, :]
    is_fwd = lax.broadcasted_iota(jnp.int32, (1, P2), 1) < P
    for b in range(nb):
        hr = jnp.zeros((1, P2), f32)
        hi = jnp.zeros((1, P2), f32)
        for n in range(nc):
            rf = b * nc + n
            rb = b * nc + (nc - 1 - n)
            hin_ref[rf:rf + 1, 0:P] = hr[:, 0:P]
            hin_ref[rb:rb + 1, P:P2] = hr[:, P:P2]
            hin_ref[rf:rf + 1, P2:P2 + P] = hi[:, 0:P]
            hin_ref[rb:rb + 1, P2 + P:2 * P2] = hi[:, P:P2]
            cr = jnp.where(is_fwd, cst_ref[rf:rf + 1, 0:P2], cst_ref[rb:rb + 1, 0:P2])
            ci = jnp.where(is_fwd, cst_ref[rf:rf + 1, P2:2 * P2], cst_ref[rb:rb + 1, P2:2 * P2])
            hr, hi = aqr * hr - aqi * hi + cr, aqr * hi + aqi * hr + ci

    hin = hin_ref[...].astype(bf16)
    for cp in range(C // 2):
        cols = slice(cp * 2 * Q, (cp + 1) * 2 * Q)
        y2 = (jnp.dot(x, t_ref[:, cols], preferred_element_type=f32)
              + jnp.dot(hin, wc_ref[0, :, cols], preferred_element_type=f32))
        for h in range(2):
            c = 2 * cp + h
            y = y2[:, h * Q:(h + 1) * Q] + dsk_ref[0, c:c + 1, :] * ut_ref[c].astype(f32)
            o_ref[c] = jax.nn.gelu(y)


def _s5(ut, nb, nc, kfull, wb, wc, coef, d_skip):
    rows = nb * nc
    ut3 = ut.reshape(SSM_WIDTH, rows, Q)
    dsk = jnp.broadcast_to(d_skip.astype(f32).reshape(SSM_GROUPS, SSM_GROUP, 1), (SSM_GROUPS, SSM_GROUP, Q))
    C = SSM_GROUP
    out = pl.pallas_call(
        functools.partial(_s5_kernel, nb, nc),
        out_shape=jax.ShapeDtypeStruct((SSM_WIDTH, rows, Q), f32),
        grid=(SSM_GROUPS,),
        in_specs=[
            pl.BlockSpec((C, rows, Q), lambda g: (g, 0, 0)),
            pl.BlockSpec((1, C, C, 2 * Q), lambda g: (g, 0, 0, 0)),
            pl.BlockSpec((1, GW, 4 * SSM_STATE), lambda g: (g, 0, 0)),
            pl.BlockSpec((1, 4 * SSM_STATE, GW), lambda g: (g, 0, 0)),
            pl.BlockSpec((1, 8, 2 * SSM_STATE), lambda g: (g, 0, 0)),
            pl.BlockSpec((1, C, Q), lambda g: (g, 0, 0)),
        ],
        out_specs=pl.BlockSpec((C, rows, Q), lambda g: (g, 0, 0)),
        scratch_shapes=[
            pltpu.VMEM((rows, GW), bf16),
            pltpu.VMEM((GW, GW), bf16),
            pltpu.VMEM((rows, 4 * SSM_STATE), f32),
            pltpu.VMEM((rows, 4 * SSM_STATE), f32),
        ],
        compiler_params=_cparams(("parallel",), VMEM_LIMIT),
        name="s5_mixer",
    )(ut3, kfull, wb, wc, coef, dsk)
    return out.reshape(SSM_WIDTH, rows * Q)


def _alibi_slopes():
    return [float(2.0 ** (-8.0 * (h + 1) / N_HEADS)) for h in range(N_HEADS)]


def _attn_kernel(q_ref, km_ref, kp_ref, ko_ref, kn_ref, sink_ref, o_ref):
    jq = pl.program_id(1)
    nq = pl.num_programs(1)
    kv = jnp.concatenate([km_ref[...], kp_ref[...], ko_ref[...], kn_ref[...]], axis=0)
    nk = 4 * BLOCK
    row = lax.broadcasted_iota(jnp.int32, (BLOCK, nk), 0)
    col = lax.broadcasted_iota(jnp.int32, (BLOCK, nk), 1)
    seg = col // BLOCK
    cc = col % BLOCK
    rel = (seg - 2) * BLOCK + cc - row
    dist = jnp.abs(rel)
    band = (seg >= 1) & (dist <= WINDOW)
    band = band & jnp.logical_not((seg == 1) & (jq == 0))
    band = band & jnp.logical_not((seg == 3) & (jq == nq - 1))
    meta = (seg == 0) & (cc >= BLOCK - N_META)
    distf = dist.astype(f32)
    slopes = _alibi_slopes()
    grp = N_HEADS // N_KV_HEADS
    for h in range(N_HEADS):
        g = h // grp
        qh = q_ref[:, h * HEAD_DIM:(h + 1) * HEAD_DIM]
        kh = kv[:, g * HEAD_DIM:(g + 1) * HEAD_DIM]
        vh = kv[:, KV_WIDTH + g * HEAD_DIM:KV_WIDTH + (g + 1) * HEAD_DIM]
        s = _dot_nt(qh, kh)
        s = jnp.where(band, s - slopes[h] * distf, jnp.where(meta, s, NEG_INF))
        sink = sink_ref[h]
        m = jnp.maximum(jnp.max(s, axis=1, keepdims=True), sink)
        p = jnp.exp(s - m)
        den = jnp.sum(p, axis=1, keepdims=True) + jnp.exp(sink - m)
        o = jnp.dot(p.astype(bf16), vh, preferred_element_type=f32) / den
        o_ref[:, h * HEAD_DIM:(h + 1) * HEAD_DIM] = o.astype(bf16)


def _attention(qs, kvs, sink, nb, nq):
    pb = PAD_FRONT // BLOCK
    nbp = nq + pb
    kv_spec = lambda f: pl.BlockSpec((BLOCK, 2 * KV_WIDTH), f)
    return pl.pallas_call(
        _attn_kernel,
        out_shape=jax.ShapeDtypeStruct((nb * nq * BLOCK, ATTN_WIDTH), bf16),
        grid=(nb, nq),
        in_specs=[
            pl.BlockSpec((BLOCK, ATTN_WIDTH), lambda b, j: (b * nbp + pb + j, 0)),
            kv_spec(lambda b, j: (b * nbp + pb - 1, 0)),
            kv_spec(lambda b, j: (b * nbp + pb + j - 1, 0)),
            kv_spec(lambda b, j: (b * nbp + pb + j, 0)),
            kv_spec(lambda b, j: (b * nbp + jnp.minimum(pb + j + 1, nbp - 1), 0)),
            pl.BlockSpec(memory_space=pltpu.SMEM),
        ],
        out_specs=pl.BlockSpec((BLOCK, ATTN_WIDTH), lambda b, j: (b * nq + j, 0)),
        compiler_params=_cparams(("parallel", "parallel")),
        name="window_attn",
    )(qs, kvs, kvs, kvs, kvs, sink.astype(f32))


MG_TM = 256


def _merge_kernel(yt_ref, yb_ref, g0_ref, g1_ref, wglu_ref, bglu_ref, wps_ref, wpa_ref, o_ref):
    y = yt_ref[...].T
    gl = jnp.dot(y.astype(bf16), wglu_ref[...], preferred_element_type=f32) + bglu_ref[...]
    ya = y * jax.nn.sigmoid(gl)
    pa = jnp.dot(ya.astype(bf16), wps_ref[...], preferred_element_type=f32)
    pbv = jnp.dot(yb_ref[...], wpa_ref[...], preferred_element_type=f32)
    o_ref[...] = (g0_ref[...] * pa + g1_ref[...] * pbv).astype(bf16)


def _merge(yt, yb, gates, w_glu, b_glu, w_ps, w_pa, nb, seq):
    tm = MG_TM
    nt = seq // tm
    lp_t = (seq + PAD_FRONT) // tm
    off = PAD_FRONT // tm
    pad_idx = lambda b, i: b * lp_t + off + i
    return pl.pallas_call(
        _merge_kernel,
        out_shape=jax.ShapeDtypeStruct((nb * seq, D_MODEL), bf16),
        grid=(nb, nt),
        in_specs=[
            pl.BlockSpec((SSM_WIDTH, tm), lambda b, i: (0, pad_idx(b, i))),
            pl.BlockSpec((tm, ATTN_WIDTH), lambda b, i: (b * nt + i, 0)),
            pl.BlockSpec((tm, D_MODEL), lambda b, i: (pad_idx(b, i), 0)),
            pl.BlockSpec((tm, D_MODEL), lambda b, i: (pad_idx(b, i), 1)),
            pl.BlockSpec((SSM_WIDTH, SSM_WIDTH), lambda b, i: (0, 0)),
            pl.BlockSpec((1, SSM_WIDTH), lambda b, i: (0, 0)),
            pl.BlockSpec((SSM_WIDTH, D_MODEL), lambda b, i: (0, 0)),
            pl.BlockSpec((ATTN_WIDTH, D_MODEL), lambda b, i: (0, 0)),
        ],
        out_specs=pl.BlockSpec((tm, D_MODEL), lambda b, i: (b * nt + i, 0)),
        compiler_params=_cparams(("parallel", "parallel"), VMEM_LIMIT),
        name="glu_merge",
    )(yt, yb, gates, gates, w_glu.astype(bf16), b_glu.astype(f32)[None], w_ps.astype(bf16), w_pa.astype(bf16))


def _outproj_kernel(x_ref, m_ref, w_ref, o_ref):
    o_ref[...] = x_ref[...] + jnp.dot(m_ref[...], w_ref[...], preferred_element_type=f32)


def _outproj(x2d, merged, w_out):
    tm = 512
    rows = x2d.shape[0]
    return pl.pallas_call(
        _outproj_kernel,
        out_shape=jax.ShapeDtypeStruct((rows, D_MODEL), f32),
        grid=(rows // tm,),
        in_specs=[
            pl.BlockSpec((tm, D_MODEL), lambda i: (i, 0)),
            pl.BlockSpec((tm, D_MODEL), lambda i: (i, 0)),
            pl.BlockSpec((D_MODEL, D_MODEL), lambda i: (0, 0)),
        ],
        out_specs=pl.BlockSpec((tm, D_MODEL), lambda i: (i, 0)),
        compiler_params=_cparams(("parallel",), VMEM_LIMIT),
        name="out_proj",
    )(x2d, merged, w_out.astype(bf16))


RT_TM = 256


def _topk_rows(s, payload, vals_ref, pay_ref, r0):
    n = s.shape[0]
    rowid = lax.broadcasted_iota(jnp.int32, s.shape, 0)
    for i in range(PEER_TOPK):
        m = jnp.max(s, axis=0, keepdims=True)
        idx = jnp.min(jnp.where(s == m, rowid, n), axis=0, keepdims=True)
        sel = rowid == idx
        vals_ref[i:i + 1, :] = m
        if payload is None:
            pay_ref[r0 + i:r0 + i + 1, :] = idx
        else:
            pay_ref[r0 + i:r0 + i + 1, :] = jnp.max(jnp.where(sel, payload, -1), axis=0, keepdims=True)
        s = jnp.where(sel, -jnp.inf, s)


def _route_kernel(hs_ref, g_ref, wq_ref, k1_ref, k2_ref, h2_ref, eidx_ref, gate_ref,
                  v1_ref, i1_ref, v2_ref, i2_ref, cand_ref, cidx_ref, sc_ref):
    h2 = _rms(hs_ref[...], g_ref[...])
    h2_ref[...] = h2
    qf = jnp.dot(h2.astype(bf16), wq_ref[...], preferred_element_type=f32).astype(bf16)
    K = PEER_TOPK
    for h in range(PEER_HEADS):
        q1 = qf[:, h * 2 * PEER_HALF:h * 2 * PEER_HALF + PEER_HALF]
        q2 = qf[:, h * 2 * PEER_HALF + PEER_HALF:(h + 1) * 2 * PEER_HALF]
        s1 = _dot_nt(k1_ref[h], q1)
        s2 = _dot_nt(k2_ref[h], q2)
        _topk_rows(s1, None, v1_ref, i1_ref, 0)
        _topk_rows(s2, None, v2_ref, i2_ref, 0)
        v2 = v2_ref[...]
        i2 = i2_ref[...]
        for a in range(K):
            cand_ref[a * K:(a + 1) * K, :] = v1_ref[a:a + 1, :] + v2
            cidx_ref[a * K:(a + 1) * K, :] = i1_ref[a:a + 1, :] * PEER_KEYS + i2
        _topk_rows(cand_ref[...], cidx_ref[...], sc_ref, eidx_ref, h * K)
        sc = sc_ref[...]
        e = jnp.exp(sc - sc[0:1, :])
        gate_ref[h * K:(h + 1) * K, :] = e / jnp.sum(e, axis=0, keepdims=True)


def _route(hs2, norm_g, w_q, k1, k2):
    rows = hs2.shape[0]
    tm = RT_TM
    K = PEER_TOPK
    return pl.pallas_call(
        _route_kernel,
        out_shape=(
            jax.ShapeDtypeStruct((rows, D_MODEL), f32),
            jax.ShapeDtypeStruct((PEER_SLOTS, rows), jnp.int32),
            jax.ShapeDtypeStruct((PEER_SLOTS, rows), f32),
        ),
        grid=(rows // tm,),
        in_specs=[
            pl.BlockSpec((tm, D_MODEL), lambda i: (i, 0)),
            pl.BlockSpec((1, D_MODEL), lambda i: (0, 0)),
            pl.BlockSpec((D_MODEL, PEER_HEADS * 2 * PEER_HALF), lambda i: (0, 0)),
            pl.BlockSpec((PEER_HEADS, PEER_KEYS, PEER_HALF), lambda i: (0, 0, 0)),
            pl.BlockSpec((PEER_HEADS, PEER_KEYS, PEER_HALF), lambda i: (0, 0, 0)),
        ],
        out_specs=(
            pl.BlockSpec((tm, D_MODEL), lambda i: (i, 0)),
            pl.BlockSpec((PEER_SLOTS, tm), lambda i: (0, i)),
            pl.BlockSpec((PEER_SLOTS, tm), lambda i: (0, i)),
        ),
        scratch_shapes=[
            pltpu.VMEM((K, tm), f32), pltpu.VMEM((K, tm), jnp.int32),
            pltpu.VMEM((K, tm), f32), pltpu.VMEM((K, tm), jnp.int32),
            pltpu.VMEM((K * K, tm), f32), pltpu.VMEM((K * K, tm), jnp.int32),
            pltpu.VMEM((K, tm), f32),
        ],
        compiler_params=_cparams(("parallel",), VMEM_LIMIT),
        name="peer_route",
    )(hs2, norm_g.astype(f32)[None], w_q.astype(bf16), k1.astype(bf16), k2.astype(bf16))


def _pack_kernel(u_ref, v_ref, o_ref):
    ub = pltpu.bitcast(u_ref[...].astype(bf16).astype(f32), jnp.uint32)
    vb = pltpu.bitcast(v_ref[...].astype(bf16).astype(f32), jnp.uint32)
    o_ref[...] = ub | (vb >> 16)


def _pack_tables(u_emb, v_emb):
    n = u_emb.shape[0]
    tm = 512
    spec = pl.BlockSpec((tm, D_MODEL), lambda i: (i, 0))
    return pl.pallas_call(
        _pack_kernel,
        out_shape=jax.ShapeDtypeStruct((n, D_MODEL), jnp.uint32),
        grid=(n // tm,),
        in_specs=[spec, spec],
        out_specs=spec,
        compiler_params=_cparams(("parallel",)),
        name="peer_pack",
    )(u_emb, v_emb)


GT_TQ = 8
GT_ROWS = GT_TQ * PEER_SLOTS


def _gather_copy(tbl_ref, buf_ref, sem_ref, slot, e, j):
    return pltpu.make_async_copy(tbl_ref.at[pl.ds(e, 1), :], buf_ref.at[slot, pl.ds(j, 1), :], sem_ref.at[slot])


def _expert_kernel(idx_ref, idxn_ref, gate_ref, h2_ref, hs_ref, gf_ref, tbl_ref, o_ref, buf_ref, sem_ref, acc_ref):
    i = pl.program_id(0)
    nt = pl.num_programs(0)
    slot = i % 2

    def issue(src_ref, dst_slot):
        def body(j, carry):
            _gather_copy(tbl_ref, buf_ref, sem_ref, dst_slot, src_ref[0, 0, j], j).start()
            return carry
        lax.fori_loop(0, GT_ROWS, body, 0, unroll=8)

    @pl.when(i == 0)
    def _():
        issue(idx_ref, 0)

    @pl.when(i + 1 < nt)
    def _():
        issue(idxn_ref, 1 - slot)

    pltpu.make_async_copy(tbl_ref.at[pl.ds(0, GT_ROWS), :], buf_ref.at[slot], sem_ref.at[slot]).wait()

    gate = gate_ref[0]
    for t in range(GT_TQ):
        w32 = buf_ref[slot, t * PEER_SLOTS:(t + 1) * PEER_SLOTS, :]
        u = pltpu.bitcast(w32 & jnp.uint32(0xFFFF0000), f32)
        a = jnp.sum(u * h2_ref[t:t + 1, :], axis=1, keepdims=True)
        w = gate[:, t:t + 1] * jax.nn.gelu(a)
        v = pltpu.bitcast(w32 << 16, f32)
        acc_ref[t:t + 1, :] = jnp.sum(v * w, axis=0, keepdims=True)
    o_ref[...] = _rms(hs_ref[...] + acc_ref[...], gf_ref[...])


def _experts(eidx, gate, h2, hs2, norm_g, tbl):
    rows = hs2.shape[0]
    nt = rows // GT_TQ
    idx3 = eidx.T.reshape(nt, 1, GT_ROWS)
    gate3 = jnp.transpose(gate.reshape(PEER_SLOTS, nt, GT_TQ), (1, 0, 2))
    row_spec = pl.BlockSpec((GT_TQ, D_MODEL), lambda i: (i, 0))
    return pl.pallas_call(
        _expert_kernel,
        out_shape=jax.ShapeDtypeStruct((rows, D_MODEL), f32),
        grid=(nt,),
        in_specs=[
            pl.BlockSpec((1, 1, GT_ROWS), lambda i: (i, 0, 0), memory_space=pltpu.SMEM),
            pl.BlockSpec((1, 1, GT_ROWS), lambda i: (jnp.minimum(i + 1, nt - 1), 0, 0), memory_space=pltpu.SMEM),
            pl.BlockSpec((1, PEER_SLOTS, GT_TQ), lambda i: (i, 0, 0)),
            row_spec,
            row_spec,
            pl.BlockSpec((1, D_MODEL), lambda i: (0, 0)),
            pl.BlockSpec(memory_space=pl.ANY),
        ],
        out_specs=row_spec,
        scratch_shapes=[
            pltpu.VMEM((2, GT_ROWS, D_MODEL), jnp.uint32),
            pltpu.SemaphoreType.DMA((2,)),
            pltpu.VMEM((GT_TQ, D_MODEL), f32),
        ],
        compiler_params=_cparams(("arbitrary",), VMEM_LIMIT),
        name="peer_experts",
    )(idx3, idx3, gate3, h2, hs2, norm_g.astype(f32)[None], tbl)


def kernel(x, meta_tokens, norm_mix, w_in, b_gate, ssm_lam_re, ssm_lam_im, ssm_log_dt, ssm_b_re, ssm_b_im, ssm_c_re, ssm_c_im, ssm_d, ssm_w_glu, ssm_b_glu, attn_sink, w_proj_ssm, w_proj_attn, w_out, norm_ffn, peer_w_q, peer_k1, peer_k2, peer_u, peer_v, norm_final):
    nb, seq, d = x.shape
    assert w_in.shape[0] == 1 and d == D_MODEL and seq % 256 == 0 and (nb * (seq + PAD_FRONT)) % 512 == 0
    l = 0
    lp = seq + PAD_FRONT
    nc = lp // Q

    front = jnp.concatenate([jnp.zeros((PAD_FRONT - N_META, d), x.dtype), meta_tokens.astype(x.dtype)], axis=0)
    hs_pad = jnp.concatenate([jnp.broadcast_to(front[None], (nb, PAD_FRONT, d)), x], axis=1).reshape(nb * lp, d)

    ut, qs, kvs, gates = _inproj(hs_pad, norm_mix[l], w_in[l], b_gate[l])

    kfull, wb, wc, coef = _s5_weights(ssm_lam_re[l], ssm_lam_im[l], ssm_log_dt[l], ssm_b_re[l], ssm_b_im[l],
                                      ssm_c_re[l], ssm_c_im[l])
    yt = _s5(ut, nb, nc, kfull, wb, wc, coef, ssm_d[l])

    yb = _attention(qs, kvs, attn_sink[l], nb, seq // BLOCK)

    merged = _merge(yt, yb, gates, ssm_w_glu[l], ssm_b_glu[l], w_proj_ssm[l], w_proj_attn[l], nb, seq)
    hs2 = _outproj(x.reshape(nb * seq, d), merged, w_out[l])

    h2, eidx, gate = _route(hs2, norm_ffn[l], peer_w_q[l], peer_k1[l], peer_k2[l])
    tbl = _pack_tables(peer_u[l], peer_v[l])
    out = _experts(eidx, gate, h2, hs2, norm_final, tbl)
    return out.reshape(nb, seq, d)
```

```python
import functools
import math

import numpy as np
import jax
import jax.numpy as jnp
from jax import lax
from jax.experimental import pallas as pl
from jax.experimental.pallas import tpu as pltpu

f32 = jnp.float32
bf16 = jnp.bfloat16

D_MODEL = 2048
N_META = 16
SSM_WIDTH = 1024
SSM_GROUP = 16
SSM_GROUPS = 64
SSM_STATE = 64
HEAD_DIM = 128
N_HEADS = 8
N_KV_HEADS = 2
ATTN_WIDTH = 1024
KV_WIDTH = 256
WINDOW = 128
BLOCK = 128
PEER_HEADS = 8
PEER_KEYS = 128
PEER_HALF = 128
PEER_TOPK = 16
PEER_SLOTS = PEER_HEADS * PEER_TOPK
NORM_EPS = 1e-6
NEG_INF = -1e30

PAD_FRONT = 256
VMEM_LIMIT = 56 * 1024 * 1024

HI = lax.Precision.HIGHEST


def _cparams(sem, vmem=None):
    return pltpu.CompilerParams(dimension_semantics=sem, vmem_limit_bytes=vmem)


def _rms(x, g):
    ms = jnp.mean(x * x, axis=-1, keepdims=True)
    return x * lax.rsqrt(ms + NORM_EPS) * g


def _dot_nt(a, b):
    return lax.dot_general(a, b, (((1,), (1,)), ((), ())), preferred_element_type=f32)


IN_TM = 512
IN_TN = 512
_IN_QJ = ATTN_WIDTH // IN_TN
_IN_KVJ = _IN_QJ + (2 * KV_WIDTH) // IN_TN


def _inproj_kernel(x_ref, g_ref, wut_ref, w2_ref, b2_ref, ut_ref, q_ref, kv_ref, gt_ref, xn_ref):
    j = pl.program_id(1)

    @pl.when(j == 0)
    def _():
        xn = _rms(x_ref[...], g_ref[...]).astype(bf16)
        xn_ref[...] = xn
        ut_ref[...] = _dot_nt(wut_ref[...], xn).astype(bf16)

    acc = jnp.dot(xn_ref[...], w2_ref[...], preferred_element_type=f32)

    @pl.when(j < _IN_QJ)
    def _():
        q_ref[...] = (acc * (HEAD_DIM ** -0.5)).astype(bf16)

    @pl.when((j >= _IN_QJ) & (j < _IN_KVJ))
    def _():
        kv_ref[...] = acc.astype(bf16)

    @pl.when(j >= _IN_KVJ)
    def _():
        gt_ref[...] = jax.nn.sigmoid(acc + b2_ref[...])


def _inproj(hs_pad, norm_g, w_in, b_gate):
    rows = hs_pad.shape[0]
    wut = w_in[:, :SSM_WIDTH].T.astype(bf16)
    w2 = w_in[:, SSM_WIDTH:].astype(bf16)
    n2 = w2.shape[1]
    b2 = jnp.concatenate([jnp.zeros((n2 - b_gate.shape[0],), f32), b_gate.astype(f32)])[None]
    nj = n2 // IN_TN
    ngate = 2 * D_MODEL
    return pl.pallas_call(
        _inproj_kernel,
        out_shape=(
            jax.ShapeDtypeStruct((SSM_WIDTH, rows), bf16),
            jax.ShapeDtypeStruct((rows, ATTN_WIDTH), bf16),
            jax.ShapeDtypeStruct((rows, 2 * KV_WIDTH), bf16),
            jax.ShapeDtypeStruct((rows, ngate), f32),
        ),
        grid=(rows // IN_TM, nj),
        in_specs=[
            pl.BlockSpec((IN_TM, D_MODEL), lambda i, j: (i, 0)),
            pl.BlockSpec((1, D_MODEL), lambda i, j: (0, 0)),
            pl.BlockSpec((SSM_WIDTH, D_MODEL), lambda i, j: (0, 0)),
            pl.BlockSpec((D_MODEL, IN_TN), lambda i, j: (0, j)),
            pl.BlockSpec((1, IN_TN), lambda i, j: (0, j)),
        ],
        out_specs=(
            pl.BlockSpec((SSM_WIDTH, IN_TM), lambda i, j: (0, i)),
            pl.BlockSpec((IN_TM, IN_TN), lambda i, j: (i, jnp.minimum(j, _IN_QJ - 1))),
            pl.BlockSpec((IN_TM, IN_TN), lambda i, j: (i, 0)),
            pl.BlockSpec((IN_TM, IN_TN), lambda i, j: (i, jnp.maximum(j - _IN_KVJ, 0))),
        ),
        scratch_shapes=[pltpu.VMEM((IN_TM, D_MODEL), bf16)],
        compiler_params=_cparams(("parallel", "arbitrary"), VMEM_LIMIT),
        name="inproj",
    )(hs_pad, norm_g.astype(f32)[None], wut, w2, b2)


Q = BLOCK
GW = SSM_GROUP * Q


def _s5_weights(lam_re, lam_im, log_dt, b_re, b_im, c_re, c_im):
    dt = jnp.exp(log_dt.astype(f32))[..., None]
    lr = lam_re.astype(f32)
    li = lam_im.astype(f32)
    k = jnp.arange(Q + 1, dtype=f32)
    mag = jnp.exp(lr[..., None] * dt[..., None] * k)
    ang = li[..., None] * dt[..., None] * k
    ar = mag * jnp.cos(ang)
    ai = mag * jnp.sin(ang)
    a1r, a1i = ar[..., 1], ai[..., 1]
    den = lr * lr + li * li
    nr, ni = a1r - 1.0, a1i
    cr = (nr * lr + ni * li) / den
    ci = (ni * lr - nr * li) / den
    br, bi = b_re.astype(f32), b_im.astype(f32)
    bbr = cr[..., None] * br - ci[..., None] * bi
    bbi = cr[..., None] * bi + ci[..., None] * br
    ccr = jnp.swapaxes(c_re.astype(f32), -1, -2)
    cci = jnp.swapaxes(c_im.astype(f32), -1, -2)

    cbr = ccr[..., :, None] * bbr[..., None, :] - cci[..., :, None] * bbi[..., None, :]
    cbi = ccr[..., :, None] * bbi[..., None, :] + cci[..., :, None] * bbr[..., None, :]
    G, P, C = cbr.shape[1], cbr.shape[2], cbr.shape[3]
    cbr2 = cbr.reshape(2, G, P, C * C)
    cbi2 = cbi.reshape(2, G, P, C * C)
    kk = (jnp.einsum('dgpk,dgpx->dgkx', ar[..., :Q], cbr2, precision=HI)
          - jnp.einsum('dgpk,dgpx->dgkx', ai[..., :Q], cbi2, precision=HI))
    kk = kk.reshape(2, G, Q, C, C)
    kf, kb = kk[0], kk[1]
    lag0 = (kf[:, 0] + kb[:, 0])[:, None]
    zero = jnp.zeros_like(lag0)
    kfull = jnp.concatenate([lag0, kf[:, 1:], zero, kb[:, 1:][:, ::-1]], axis=1)
    kfull = jnp.transpose(kfull, (0, 3, 2, 1))

    def bu_pow(d, pw_r, pw_i):
        re = pw_r[:, :, None, :] * bbr[d][..., None] - pw_i[:, :, None, :] * bbi[d][..., None]
        im = pw_r[:, :, None, :] * bbi[d][..., None] + pw_i[:, :, None, :] * bbr[d][..., None]
        return jnp.transpose(re, (0, 2, 3, 1)), jnp.transpose(im, (0, 2, 3, 1))
    fr, fi = bu_pow(0, ar[0][..., :Q][..., ::-1], ai[0][..., :Q][..., ::-1])
    rr, ri = bu_pow(1, ar[1][..., :Q], ai[1][..., :Q])
    wb = jnp.concatenate([fr, rr, fi, ri], axis=-1).reshape(G, C * Q, 4 * P)

    def c_pow(d, pw_r, pw_i):
        re = ccr[d][..., None] * pw_r[:, :, None, :] - cci[d][..., None] * pw_i[:, :, None, :]
        im = ccr[d][..., None] * pw_i[:, :, None, :] + cci[d][..., None] * pw_r[:, :, None, :]
        return re, -im
    f_re, f_im = c_pow(0, ar[0][..., 1:], ai[0][..., 1:])
    b_re_, b_im_ = c_pow(1, ar[1][..., 1:][..., ::-1], ai[1][..., 1:][..., ::-1])
    wc = jnp.concatenate([f_re, b_re_, f_im, b_im_], axis=1).reshape(G, 4 * P, C * Q)

    aqr, aqi = ar[..., Q], ai[..., Q]
    coef = jnp.stack([jnp.concatenate([aqr[0], aqr[1]], axis=-1),
                      jnp.concatenate([aqi[0], aqi[1]], axis=-1)], axis=1)
    coef = jnp.concatenate([coef, jnp.zeros((G, 6, 2 * P), f32)], axis=1)
    return kfull, wb.astype(bf16), wc.astype(bf16), coef


def _s5_kernel(nb, nc, ut_ref, kf_ref, wb_ref, wc_ref, coef_ref, dsk_ref, o_ref,
               x_ref, t_ref, cst_ref, hin_ref):
    C = SSM_GROUP
    P2 = 2 * SSM_STATE
    for c in range(C):
        x_ref[:, c * Q:(c + 1) * Q] = ut_ref[c]
    x = x_ref[...]

    def build(cp, carry):
        r0 = pl.multiple_of(cp * Q, Q)
        for c in range(C):
            row = kf_ref[0, cp, pl.ds(c, 1), :]
            blk = jnp.broadcast_to(row, (Q, 2 * Q))
            rolled = pltpu.roll(blk, 0, 1, stride=1, stride_axis=0)
            t_ref[pl.ds(r0, Q), c * Q:(c + 1) * Q] = rolled[:, :Q].astype(bf16)
        return carry
    lax.fori_loop(0, C, build, 0)

    cst_ref[...] = jnp.dot(x, wb_ref[0], preferred_element_type=f32)
    P = SSM_STATE
    aqr = coef_ref[0, 0:1, :]
    aqi = coef_ref[0, 1:2, :]
    is_fwd = lax.broadcasted_iota(jnp.int32, (1, P2), 1) < P
    for b in range(nb):
        hr = jnp.zeros((1, P2), f32)
        hi = jnp.zeros((1, P2), f32)
        for n in range(nc):
            rf = b * nc + n
            rb = b * nc + (nc - 1 - n)
            hin_ref[rf:rf + 1, 0:P] = hr[:, 0:P]
            hin_ref[rb:rb + 1, P:P2] = hr[:, P:P2]
            hin_ref[rf:rf + 1, P2:P2 + P] = hi[:, 0:P]
            hin_ref[rb:rb + 1, P2 + P:2 * P2] = hi[:, P:P2]
            cr = jnp.where(is_fwd, cst_ref[rf:rf + 1, 0:P2], cst_ref[rb:rb + 1, 0:P2])
            ci = jnp.where(is_fwd, cst_ref[rf:rf + 1, P2:2 * P2], cst_ref[rb:rb + 1, P2:2 * P2])
            hr, hi = aqr * hr - aqi * hi + cr, aqr * hi + aqi * hr + ci

    hin = hin_ref[...].astype(bf16)
    for cp in range(C // 2):
        cols = slice(cp * 2 * Q, (cp + 1) * 2 * Q)
        y2 = (jnp.dot(x, t_ref[:, cols], preferred_element_type=f32)
              + jnp.dot(hin, wc_ref[0, :, cols], preferred_element_type=f32))
        for h in range(2):
            c = 2 * cp + h
            y = y2[:, h * Q:(h + 1) * Q] + dsk_ref[0, c:c + 1, :] * ut_ref[c].astype(f32)
            o_ref[c] = jax.nn.gelu(y)


def _s5(ut, nb, nc, kfull, wb, wc, coef, d_skip):
    rows = nb * nc
    ut3 = ut.reshape(SSM_WIDTH, rows, Q)
    dsk = jnp.broadcast_to(d_skip.astype(f32).reshape(SSM_GROUPS, SSM_GROUP, 1), (SSM_GROUPS, SSM_GROUP, Q))
    C = SSM_GROUP
    out = pl.pallas_call(
        functools.partial(_s5_kernel, nb, nc),
        out_shape=jax.ShapeDtypeStruct((SSM_WIDTH, rows, Q), f32),
        grid=(SSM_GROUPS,),
        in_specs=[
            pl.BlockSpec((C, rows, Q), lambda g: (g, 0, 0)),
            pl.BlockSpec((1, C, C, 2 * Q), lambda g: (g, 0, 0, 0)),
            pl.BlockSpec((1, GW, 4 * SSM_STATE), lambda g: (g, 0, 0)),
            pl.BlockSpec((1, 4 * SSM_STATE, GW), lambda g: (g, 0, 0)),
            pl.BlockSpec((1, 8, 2 * SSM_STATE), lambda g: (g, 0, 0)),
            pl.BlockSpec((1, C, Q), lambda g: (g, 0, 0)),
        ],
        out_specs=pl.BlockSpec((C, rows, Q), lambda g: (g, 0, 0)),
        scratch_shapes=[
            pltpu.VMEM((rows, GW), bf16),
            pltpu.VMEM((GW, GW), bf16),
            pltpu.VMEM((rows, 4 * SSM_STATE), f32),
            pltpu.VMEM((rows, 4 * SSM_STATE), f32),
        ],
        compiler_params=_cparams(("parallel",), VMEM_LIMIT),
        name="s5_mixer",
    )(ut3, kfull, wb, wc, coef, dsk)
    return out.reshape(SSM_WIDTH, rows * Q)


def _alibi_slopes():
    return [float(2.0 ** (-8.0 * (h + 1) / N_HEADS)) for h in range(N_HEADS)]


def _attn_kernel(q_ref, km_ref, kp_ref, ko_ref, kn_ref, sink_ref, o_ref):
    jq = pl.program_id(1)
    nq = pl.num_programs(1)
    kv = jnp.concatenate([km_ref[...], kp_ref[...], ko_ref[...], kn_ref[...]], axis=0)
    nk = 4 * BLOCK
    row = lax.broadcasted_iota(jnp.int32, (BLOCK, nk), 0)
    col = lax.broadcasted_iota(jnp.int32, (BLOCK, nk), 1)
    seg = col // BLOCK
    cc = col % BLOCK
    rel = (seg - 2) * BLOCK + cc - row
    dist = jnp.abs(rel)
    band = (seg >= 1) & (dist <= WINDOW)
    band = band & jnp.logical_not((seg == 1) & (jq == 0))
    band = band & jnp.logical_not((seg == 3) & (jq == nq - 1))
    meta = (seg == 0) & (cc >= BLOCK - N_META)
    distf = dist.astype(f32)
    slopes = _alibi_slopes()
    grp = N_HEADS // N_KV_HEADS
    for h in range(N_HEADS):
        g = h // grp
        qh = q_ref[:, h * HEAD_DIM:(h + 1) * HEAD_DIM]
        kh = kv[:, g * HEAD_DIM:(g + 1) * HEAD_DIM]
        vh = kv[:, KV_WIDTH + g * HEAD_DIM:KV_WIDTH + (g + 1) * HEAD_DIM]
        s = _dot_nt(qh, kh)
        s = jnp.where(band, s - slopes[h] * distf, jnp.where(meta, s, NEG_INF))
        sink = sink_ref[h]
        m = jnp.maximum(jnp.max(s, axis=1, keepdims=True), sink)
        p = jnp.exp(s - m)
        den = jnp.sum(p, axis=1, keepdims=True) + jnp.exp(sink - m)
        o = jnp.dot(p.astype(bf16), vh, preferred_element_type=f32) / den
        o_ref[:, h * HEAD_DIM:(h + 1) * HEAD_DIM] = o.astype(bf16)


def _attention(qs, kvs, sink, nb, nq):
    pb = PAD_FRONT // BLOCK
    nbp = nq + pb
    kv_spec = lambda f: pl.BlockSpec((BLOCK, 2 * KV_WIDTH), f)
    return pl.pallas_call(
        _attn_kernel,
        out_shape=jax.ShapeDtypeStruct((nb * nq * BLOCK, ATTN_WIDTH), bf16),
        grid=(nb, nq),
        in_specs=[
            pl.BlockSpec((BLOCK, ATTN_WIDTH), lambda b, j: (b * nbp + pb + j, 0)),
            kv_spec(lambda b, j: (b * nbp + pb - 1, 0)),
            kv_spec(lambda b, j: (b * nbp + pb + j - 1, 0)),
            kv_spec(lambda b, j: (b * nbp + pb + j, 0)),
            kv_spec(lambda b, j: (b * nbp + jnp.minimum(pb + j + 1, nbp - 1), 0)),
            pl.BlockSpec(memory_space=pltpu.SMEM),
        ],
        out_specs=pl.BlockSpec((BLOCK, ATTN_WIDTH), lambda b, j: (b * nq + j, 0)),
        compiler_params=_cparams(("parallel", "parallel")),
        name="window_attn",
    )(qs, kvs, kvs, kvs, kvs, sink.astype(f32))


MG_TM = 256


def _merge_kernel(yt_ref, yb_ref, g0_ref, g1_ref, wglu_ref, bglu_ref, wps_ref, wpa_ref, o_ref):
    y = yt_ref[...].T
    gl = jnp.dot(y.astype(bf16), wglu_ref[...], preferred_element_type=f32) + bglu_ref[...]
    ya = y * jax.nn.sigmoid(gl)
    pa = jnp.dot(ya.astype(bf16), wps_ref[...], preferred_element_type=f32)
    pbv = jnp.dot(yb_ref[...], wpa_ref[...], preferred_element_type=f32)
    o_ref[...] = (g0_ref[...] * pa + g1_ref[...] * pbv).astype(bf16)


def _merge(yt, yb, gates, w_glu, b_glu, w_ps, w_pa, nb, seq):
    tm = MG_TM
    nt = seq // tm
    lp_t = (seq + PAD_FRONT) // tm
    off = PAD_FRONT // tm
    pad_idx = lambda b, i: b * lp_t + off + i
    return pl.pallas_call(
        _merge_kernel,
        out_shape=jax.ShapeDtypeStruct((nb * seq, D_MODEL), bf16),
        grid=(nb, nt),
        in_specs=[
            pl.BlockSpec((SSM_WIDTH, tm), lambda b, i: (0, pad_idx(b, i))),
            pl.BlockSpec((tm, ATTN_WIDTH), lambda b, i: (b * nt + i, 0)),
            pl.BlockSpec((tm, D_MODEL), lambda b, i: (pad_idx(b, i), 0)),
            pl.BlockSpec((tm, D_MODEL), lambda b, i: (pad_idx(b, i), 1)),
            pl.BlockSpec((SSM_WIDTH, SSM_WIDTH), lambda b, i: (0, 0)),
            pl.BlockSpec((1, SSM_WIDTH), lambda b, i: (0, 0)),
            pl.BlockSpec((SSM_WIDTH, D_MODEL), lambda b, i: (0, 0)),
            pl.BlockSpec((ATTN_WIDTH, D_MODEL), lambda b, i: (0, 0)),
        ],
        out_specs=pl.BlockSpec((tm, D_MODEL), lambda b, i: (b * nt + i, 0)),
        compiler_params=_cparams(("parallel", "parallel"), VMEM_LIMIT),
        name="glu_merge",
    )(yt, yb, gates, gates, w_glu.astype(bf16), b_glu.astype(f32)[None], w_ps.astype(bf16), w_pa.astype(bf16))


def _outproj_kernel(x_ref, m_ref, w_ref, o_ref):
    o_ref[...] = x_ref[...] + jnp.dot(m_ref[...], w_ref[...], preferred_element_type=f32)


def _outproj(x2d, merged, w_out):
    tm = 512
    rows = x2d.shape[0]
    return pl.pallas_call(
        _outproj_kernel,
        out_shape=jax.ShapeDtypeStruct((rows, D_MODEL), f32),
        grid=(rows // tm,),
        in_specs=[
            pl.BlockSpec((tm, D_MODEL), lambda i: (i, 0)),
            pl.BlockSpec((tm, D_MODEL), lambda i: (i, 0)),
            pl.BlockSpec((D_MODEL, D_MODEL), lambda i: (0, 0)),
        ],
        out_specs=pl.BlockSpec((tm, D_MODEL), lambda i: (i, 0)),
        compiler_params=_cparams(("parallel",), VMEM_LIMIT),
        name="out_proj",
    )(x2d, merged, w_out.astype(bf16))


RT_TM = 256


def _topk_rows(s, order, payload, vals_ref, pay_ref, r0):
    big = jnp.int32(2 ** 30)
    for i in range(PEER_TOPK):
        m = jnp.max(s, axis=0, keepdims=True)
        idx = jnp.min(jnp.where(s == m, order, big), axis=0, keepdims=True)
        sel = order == idx
        vals_ref[i:i + 1, :] = m
        if payload is None:
            pay_ref[r0 + i:r0 + i + 1, :] = idx
        else:
            pay_ref[r0 + i:r0 + i + 1, :] = jnp.max(jnp.where(sel, payload, -1), axis=0, keepdims=True)
        s = jnp.where(sel, -jnp.inf, s)


_CAND_BLOCKS = (
    ("a", 0, 0, 16, 0), ("a", 1, 0, 8, 0), ("a", 2, 0, 8, 0), ("a", 3, 0, 8, 0),
    ("b", 0, 0, 8, 4), ("b", 1, 0, 8, 4), ("b", 2, 0, 8, 4), ("b", 0, 8, 8, 0),
)
_CAND_ROWS = sum(blk[3] for blk in _CAND_BLOCKS)


def _route_kernel(hs_ref, g_ref, wq_ref, k1_ref, k2_ref, h2_ref, eidx_ref, gate_ref,
                  v1_ref, i1_ref, v2_ref, i2_ref, cand_ref, cidx_ref, sc_ref):
    h2 = _rms(hs_ref[...], g_ref[...])
    h2_ref[...] = h2
    qf = jnp.dot(h2.astype(bf16), wq_ref[...], preferred_element_type=f32).astype(bf16)
    K = PEER_TOPK
    tm = qf.shape[0]
    keyid = lax.broadcasted_iota(jnp.int32, (PEER_KEYS, tm), 0)
    flat, dup = [], []
    for axis, fixed, first, n, drop in _CAND_BLOCKS:
        var = lax.broadcasted_iota(jnp.int32, (n, tm), 0) + first
        flat.append(fixed * K + var if axis == "a" else var * K + fixed)
        dup.append(var < drop)
    flat = jnp.concatenate(flat, axis=0)
    dup = jnp.concatenate(dup, axis=0)
    for h in range(PEER_HEADS):
        q1 = qf[:, h * 2 * PEER_HALF:h * 2 * PEER_HALF + PEER_HALF]
        q2 = qf[:, h * 2 * PEER_HALF + PEER_HALF:(h + 1) * 2 * PEER_HALF]
        s1 = _dot_nt(k1_ref[h], q1)
        s2 = _dot_nt(k2_ref[h], q2)
        _topk_rows(s1, keyid, None, v1_ref, i1_ref, 0)
        _topk_rows(s2, keyid, None, v2_ref, i2_ref, 0)
        r = 0
        for axis, fixed, first, n, drop in _CAND_BLOCKS:
            if axis == "a":
                va, ia = v1_ref[fixed:fixed + 1, :], i1_ref[fixed:fixed + 1, :]
                vb, ib = v2_ref[first:first + n, :], i2_ref[first:first + n, :]
            else:
                va, ia = v1_ref[first:first + n, :], i1_ref[first:first + n, :]
                vb, ib = v2_ref[fixed:fixed + 1, :], i2_ref[fixed:fixed + 1, :]
            cand_ref[r:r + n, :] = va + vb
            cidx_ref[r:r + n, :] = ia * PEER_KEYS + ib
            r += n
        cand = jnp.where(dup, -jnp.inf, cand_ref[...])
        _topk_rows(cand, flat, cidx_ref[...], sc_ref, eidx_ref, h * K)
        sc = sc_ref[...]
        e = jnp.exp(sc - sc[0:1, :])
        gate_ref[h * K:(h + 1) * K, :] = e / jnp.sum(e, axis=0, keepdims=True)


def _route(hs2, norm_g, w_q, k1, k2):
    rows = hs2.shape[0]
    tm = RT_TM
    K = PEER_TOPK
    return pl.pallas_call(
        _route_kernel,
        out_shape=(
            jax.ShapeDtypeStruct((rows, D_MODEL), f32),
            jax.ShapeDtypeStruct((PEER_SLOTS, rows), jnp.int32),
            jax.ShapeDtypeStruct((PEER_SLOTS, rows), f32),
        ),
        grid=(rows // tm,),
        in_specs=[
            pl.BlockSpec((tm, D_MODEL), lambda i: (i, 0)),
            pl.BlockSpec((1, D_MODEL), lambda i: (0, 0)),
            pl.BlockSpec((D_MODEL, PEER_HEADS * 2 * PEER_HALF), lambda i: (0, 0)),
            pl.BlockSpec((PEER_HEADS, PEER_KEYS, PEER_HALF), lambda i: (0, 0, 0)),
            pl.BlockSpec((PEER_HEADS, PEER_KEYS, PEER_HALF), lambda i: (0, 0, 0)),
        ],
        out_specs=(
            pl.BlockSpec((tm, D_MODEL), lambda i: (i, 0)),
            pl.BlockSpec((PEER_SLOTS, tm), lambda i: (0, i)),
            pl.BlockSpec((PEER_SLOTS, tm), lambda i: (0, i)),
        ),
        scratch_shapes=[
            pltpu.VMEM((K, tm), f32), pltpu.VMEM((K, tm), jnp.int32),
            pltpu.VMEM((K, tm), f32), pltpu.VMEM((K, tm), jnp.int32),
            pltpu.VMEM((_CAND_ROWS, tm), f32), pltpu.VMEM((_CAND_ROWS, tm), jnp.int32),
            pltpu.VMEM((K, tm), f32),
        ],
        compiler_params=_cparams(("parallel",), VMEM_LIMIT),
        name="peer_route",
    )(hs2, norm_g.astype(f32)[None], w_q.astype(bf16), k1.astype(bf16), k2.astype(bf16))


def _pack_kernel(u_ref, v_ref, o_ref):
    ub = pltpu.bitcast(u_ref[...].astype(bf16).astype(f32), jnp.uint32)
    vb = pltpu.bitcast(v_ref[...].astype(bf16).astype(f32), jnp.uint32)
    o_ref[...] = ub | (vb >> 16)


def _pack_tables(u_emb, v_emb):
    n = u_emb.shape[0]
    tm = 512
    spec = pl.BlockSpec((tm, D_MODEL), lambda i: (i, 0))
    return pl.pallas_call(
        _pack_kernel,
        out_shape=jax.ShapeDtypeStruct((n, D_MODEL), jnp.uint32),
        grid=(n // tm,),
        in_specs=[spec, spec],
        out_specs=spec,
        compiler_params=_cparams(("parallel",)),
        name="peer_pack",
    )(u_emb, v_emb)


GT_TQ = 8
GT_ROWS = GT_TQ * PEER_SLOTS


GT_SUB = 8
GT_LC = D_MODEL // 128
GT_RG = PEER_SLOTS // GT_SUB


def _row_copy(tbl_ref, buf_ref, sem, e, rg, s):
    return pltpu.make_async_copy(tbl_ref.at[e], buf_ref.at[rg, :, pl.ds(s, 1), :], sem)


def _expert_kernel(idx_ref, idxn_ref, gate_ref, h2_ref, hs_ref, gf_ref, tbl_ref, o_ref,
                   buf0_ref, buf1_ref, sem_ref, acc_ref):
    i = pl.program_id(0)
    nt = pl.num_programs(0)

    @pl.when(i == 0)
    def _():
        def body(j, carry):
            _row_copy(tbl_ref, buf0_ref, sem_ref.at[0], idx_ref[0, 0, j], j >> 3, j & 7).start()
            return carry
        lax.fori_loop(0, GT_ROWS, body, 0, unroll=8)

    def wait_all(buf_ref, sem):
        pltpu.make_async_copy(buf_ref, buf_ref, sem).wait()

    def step(cur_ref, cur_sem, nxt_ref, nxt_sem):
        wait_all(cur_ref, cur_sem)
        gate = gate_ref[0]
        hi_mask = jnp.uint32(0xFFFF0000)
        for t in range(GT_TQ):
            for k in range(PEER_SLOTS):
                j = t * PEER_SLOTS + k
                _row_copy(tbl_ref, nxt_ref, nxt_sem, idxn_ref[0, 0, j], j // GT_SUB, j % GT_SUB).start(priority=k % 2)
            hb = [jnp.broadcast_to(h2_ref[t:t + 1, c * 128:(c + 1) * 128], (GT_SUB, 128)) for c in range(GT_LC)]
            parts = []
            for g in range(GT_RG):
                p = None
                for c in range(GT_LC):
                    u = pltpu.bitcast(cur_ref[t * GT_RG + g, c] & hi_mask, f32) * hb[c]
                    p = u if p is None else p + u
                parts.append(p)
            a = jnp.sum(jnp.concatenate(parts, axis=0), axis=1, keepdims=True)
            w = jnp.broadcast_to(gate[:, t:t + 1] * jax.nn.gelu(a), (PEER_SLOTS, 128))
            for c in range(GT_LC):
                o = None
                for g in range(GT_RG):
                    v = pltpu.bitcast(cur_ref[t * GT_RG + g, c] << 16, f32) * w[g * GT_SUB:(g + 1) * GT_SUB]
                    o = v if o is None else o + v
                acc_ref[t:t + 1, c * 128:(c + 1) * 128] = jnp.sum(o, axis=0, keepdims=True)
        o_ref[...] = _rms(hs_ref[...] + acc_ref[...], gf_ref[...])

        @pl.when(i == nt - 1)
        def _():
            wait_all(nxt_ref, nxt_sem)

    @pl.when(i % 2 == 0)
    def _():
        step(buf0_ref, sem_ref.at[0], buf1_ref, sem_ref.at[1])

    @pl.when(i % 2 == 1)
    def _():
        step(buf1_ref, sem_ref.at[1], buf0_ref, sem_ref.at[0])


def _experts(eidx, gate, h2, hs2, norm_g, tbl):
    rows = hs2.shape[0]
    nt = rows // GT_TQ
    idx3 = eidx.T.reshape(nt, 1, GT_ROWS)
    gate3 = jnp.transpose(gate.reshape(PEER_SLOTS, nt, GT_TQ), (1, 0, 2))
    row_spec = pl.BlockSpec((GT_TQ, D_MODEL), lambda i: (i, 0))
    tbl = tbl.reshape(tbl.shape[0], GT_LC, 1, 128)
    buf = pltpu.VMEM((GT_ROWS // GT_SUB, GT_LC, GT_SUB, 128), jnp.uint32)
    return pl.pallas_call(
        _expert_kernel,
        out_shape=jax.ShapeDtypeStruct((rows, D_MODEL), f32),
        grid=(nt,),
        in_specs=[
            pl.BlockSpec((1, 1, GT_ROWS), lambda i: (i, 0, 0), memory_space=pltpu.SMEM),
            pl.BlockSpec((1, 1, GT_ROWS), lambda i: (jnp.minimum(i + 1, nt - 1), 0, 0), memory_space=pltpu.SMEM),
            pl.BlockSpec((1, PEER_SLOTS, GT_TQ), lambda i: (i, 0, 0)),
            row_spec,
            row_spec,
            pl.BlockSpec((1, D_MODEL), lambda i: (0, 0)),
            pl.BlockSpec(memory_space=pl.ANY),
        ],
        out_specs=row_spec,
        scratch_shapes=[
            buf,
            buf,
            pltpu.SemaphoreType.DMA((2,)),
            pltpu.VMEM((GT_TQ, D_MODEL), f32),
        ],
        compiler_params=_cparams(("arbitrary",), VMEM_LIMIT),
        name="peer_experts",
    )(idx3, idx3, gate3, h2, hs2, norm_g.astype(f32)[None], tbl)


def kernel(x, meta_tokens, norm_mix, w_in, b_gate, ssm_lam_re, ssm_lam_im, ssm_log_dt, ssm_b_re, ssm_b_im, ssm_c_re, ssm_c_im, ssm_d, ssm_w_glu, ssm_b_glu, attn_sink, w_proj_ssm, w_proj_attn, w_out, norm_ffn, peer_w_q, peer_k1, peer_k2, peer_u, peer_v, norm_final):
    nb, seq, d = x.shape
    assert w_in.shape[0] == 1 and d == D_MODEL and seq % 256 == 0 and (nb * (seq + PAD_FRONT)) % 512 == 0
    l = 0
    lp = seq + PAD_FRONT
    nc = lp // Q

    front = jnp.concatenate([jnp.zeros((PAD_FRONT - N_META, d), x.dtype), meta_tokens.astype(x.dtype)], axis=0)
    hs_pad = jnp.concatenate([jnp.broadcast_to(front[None], (nb, PAD_FRONT, d)), x], axis=1).reshape(nb * lp, d)

    ut, qs, kvs, gates = _inproj(hs_pad, norm_mix[l], w_in[l], b_gate[l])

    kfull, wb, wc, coef = _s5_weights(ssm_lam_re[l], ssm_lam_im[l], ssm_log_dt[l], ssm_b_re[l], ssm_b_im[l],
                                      ssm_c_re[l], ssm_c_im[l])
    yt = _s5(ut, nb, nc, kfull, wb, wc, coef, ssm_d[l])

    yb = _attention(qs, kvs, attn_sink[l], nb, seq // BLOCK)

    merged = _merge(yt, yb, gates, ssm_w_glu[l], ssm_b_glu[l], w_proj_ssm[l], w_proj_attn[l], nb, seq)
    hs2 = _outproj(x.reshape(nb * seq, d), merged, w_out[l])

    h2, eidx, gate = _route(hs2, norm_ffn[l], peer_w_q[l], peer_k1[l], peer_k2[l])
    tbl = _pack_tables(peer_u[l], peer_v[l])
    out = _experts(eidx, gate, h2, hs2, norm_final, tbl)
    return out.reshape(nb, seq, d)
```

```python
import functools
import math

import numpy as np
import jax
import jax.numpy as jnp
from jax import lax
from jax.experimental import pallas as pl
from jax.experimental.pallas import tpu as pltpu

f32 = jnp.float32
bf16 = jnp.bfloat16

D_MODEL = 2048
N_META = 16
SSM_WIDTH = 1024
SSM_GROUP = 16
SSM_GROUPS = 64
SSM_STATE = 64
HEAD_DIM = 128
N_HEADS = 8
N_KV_HEADS = 2
ATTN_WIDTH = 1024
KV_WIDTH = 256
WINDOW = 128
BLOCK = 128
PEER_HEADS = 8
PEER_KEYS = 128
PEER_HALF = 128
PEER_TOPK = 16
PEER_SLOTS = PEER_HEADS * PEER_TOPK
NORM_EPS = 1e-6
NEG_INF = -1e30

PAD_FRONT = 256
VMEM_LIMIT = 56 * 1024 * 1024

HI = lax.Precision.HIGHEST


def _cparams(sem, vmem=None):
    return pltpu.CompilerParams(dimension_semantics=sem, vmem_limit_bytes=vmem)


def _rms(x, g):
    ms = jnp.mean(x * x, axis=-1, keepdims=True)
    return x * lax.rsqrt(ms + NORM_EPS) * g


def _dot_nt(a, b):
    return lax.dot_general(a, b, (((1,), (1,)), ((), ())), preferred_element_type=f32)


def _resident(shape):
    return pl.BlockSpec(shape, lambda *_: (0,) * len(shape), pipeline_mode=pl.Buffered(1))


MX_TM = PAD_FRONT


def _mixin_kernel(x_ref, front_ref, g_ref, wut_ref, wqkv_ref, ut_ref, q_ref, kv_ref):
    xin = jnp.where(pl.program_id(1) == 0, front_ref[...], x_ref[0])
    xn = _rms(xin, g_ref[...]).astype(bf16)
    ut_ref[...] = _dot_nt(wut_ref[...], xn).astype(bf16)
    z = jnp.dot(xn, wqkv_ref[...], preferred_element_type=f32)
    q_ref[...] = (z[:, :ATTN_WIDTH] * (HEAD_DIM ** -0.5)).astype(bf16)
    kv_ref[...] = z[:, ATTN_WIDTH:].astype(bf16)


def _mixin(x, front, norm_g, w_in):
    nb, seq, d = x.shape
    nt = seq // MX_TM + 1
    rows = nb * nt * MX_TM
    wut = w_in[:, :SSM_WIDTH].T.astype(bf16)
    wqkv = w_in[:, SSM_WIDTH:SSM_WIDTH + ATTN_WIDTH + 2 * KV_WIDTH].astype(bf16)
    return pl.pallas_call(
        _mixin_kernel,
        out_shape=(
            jax.ShapeDtypeStruct((SSM_WIDTH, rows), bf16),
            jax.ShapeDtypeStruct((rows, ATTN_WIDTH), bf16),
            jax.ShapeDtypeStruct((rows, 2 * KV_WIDTH), bf16),
        ),
        grid=(nb, nt),
        in_specs=[
            pl.BlockSpec((1, MX_TM, d), lambda b, i: (b, jnp.maximum(i - 1, 0), 0)),
            _resident((MX_TM, d)),
            _resident((1, d)),
            _resident((SSM_WIDTH, d)),
            _resident((d, ATTN_WIDTH + 2 * KV_WIDTH)),
        ],
        out_specs=(
            pl.BlockSpec((SSM_WIDTH, MX_TM), lambda b, i: (0, b * nt + i)),
            pl.BlockSpec((MX_TM, ATTN_WIDTH), lambda b, i: (b * nt + i, 0)),
            pl.BlockSpec((MX_TM, 2 * KV_WIDTH), lambda b, i: (b * nt + i, 0)),
        ),
        compiler_params=_cparams(("parallel", "parallel"), VMEM_LIMIT),
        name="mixer_in",
    )(x, front, norm_g.astype(f32)[None], wut, wqkv)


GI_TM = 512
GI_TN = 2048


def _gatein_kernel(x_ref, g_ref, w_ref, b_ref, o_ref, xn_ref):
    @pl.when(pl.program_id(1) == 0)
    def _():
        xn_ref[...] = _rms(x_ref[...], g_ref[...]).astype(bf16)

    z = jnp.dot(xn_ref[...], w_ref[...], preferred_element_type=f32)
    o_ref[...] = jax.nn.sigmoid(z + b_ref[...]).astype(bf16)


def _gatein(x2d, norm_g, w_in, b_gate):
    rows, d = x2d.shape
    wg = w_in[:, SSM_WIDTH + ATTN_WIDTH + 2 * KV_WIDTH:].astype(bf16)
    ng = wg.shape[1]
    return pl.pallas_call(
        _gatein_kernel,
        out_shape=jax.ShapeDtypeStruct((rows, ng), bf16),
        grid=(rows // GI_TM, ng // GI_TN),
        in_specs=[
            pl.BlockSpec((GI_TM, d), lambda i, j: (i, 0)),
            _resident((1, d)),
            pl.BlockSpec((d, GI_TN), lambda i, j: (0, j)),
            pl.BlockSpec((1, GI_TN), lambda i, j: (0, j)),
        ],
        out_specs=pl.BlockSpec((GI_TM, GI_TN), lambda i, j: (i, j)),
        scratch_shapes=[pltpu.VMEM((GI_TM, d), bf16)],
        compiler_params=_cparams(("parallel", "arbitrary"), VMEM_LIMIT),
        name="gate_in",
    )(x2d, norm_g.astype(f32)[None], wg, b_gate.astype(f32)[None])


Q = BLOCK
GW = SSM_GROUP * Q


def _s5_weights(lam_re, lam_im, log_dt, b_re, b_im, c_re, c_im):
    dt = jnp.exp(log_dt.astype(f32))[..., None]
    lr = lam_re.astype(f32)
    li = lam_im.astype(f32)
    k = jnp.arange(Q + 1, dtype=f32)
    mag = jnp.exp(lr[..., None] * dt[..., None] * k)
    ang = li[..., None] * dt[..., None] * k
    ar = mag * jnp.cos(ang)
    ai = mag * jnp.sin(ang)
    a1r, a1i = ar[..., 1], ai[..., 1]
    den = lr * lr + li * li
    nr, ni = a1r - 1.0, a1i
    cr = (nr * lr + ni * li) / den
    ci = (ni * lr - nr * li) / den
    br, bi = b_re.astype(f32), b_im.astype(f32)
    bbr = cr[..., None] * br - ci[..., None] * bi
    bbi = cr[..., None] * bi + ci[..., None] * br
    ccr = jnp.swapaxes(c_re.astype(f32), -1, -2)
    cci = jnp.swapaxes(c_im.astype(f32), -1, -2)

    cbr = ccr[..., :, None] * bbr[..., None, :] - cci[..., :, None] * bbi[..., None, :]
    cbi = ccr[..., :, None] * bbi[..., None, :] + cci[..., :, None] * bbr[..., None, :]
    G, P, C = cbr.shape[1], cbr.shape[2], cbr.shape[3]
    cbr2 = cbr.reshape(2, G, P, C * C)
    cbi2 = cbi.reshape(2, G, P, C * C)
    kk = (jnp.einsum('dgpk,dgpx->dgkx', ar[..., :Q], cbr2, precision=HI)
          - jnp.einsum('dgpk,dgpx->dgkx', ai[..., :Q], cbi2, precision=HI))
    kk = kk.reshape(2, G, Q, C, C)
    kf, kb = kk[0], kk[1]
    lag0 = (kf[:, 0] + kb[:, 0])[:, None]
    zero = jnp.zeros_like(lag0)
    kfull = jnp.concatenate([lag0, kf[:, 1:], zero, kb[:, 1:][:, ::-1]], axis=1)
    kfull = jnp.transpose(kfull, (0, 3, 2, 1))

    def bu_pow(d, pw_r, pw_i):
        re = pw_r[:, :, None, :] * bbr[d][..., None] - pw_i[:, :, None, :] * bbi[d][..., None]
        im = pw_r[:, :, None, :] * bbi[d][..., None] + pw_i[:, :, None, :] * bbr[d][..., None]
        return jnp.transpose(re, (0, 2, 3, 1)), jnp.transpose(im, (0, 2, 3, 1))
    fr, fi = bu_pow(0, ar[0][..., :Q][..., ::-1], ai[0][..., :Q][..., ::-1])
    rr, ri = bu_pow(1, ar[1][..., :Q], ai[1][..., :Q])
    wb = jnp.concatenate([fr, rr, fi, ri], axis=-1).reshape(G, C * Q, 4 * P)

    def c_pow(d, pw_r, pw_i):
        re = ccr[d][..., None] * pw_r[:, :, None, :] - cci[d][..., None] * pw_i[:, :, None, :]
        im = ccr[d][..., None] * pw_i[:, :, None, :] + cci[d][..., None] * pw_r[:, :, None, :]
        return re, -im
    f_re, f_im = c_pow(0, ar[0][..., 1:], ai[0][..., 1:])
    b_re_, b_im_ = c_pow(1, ar[1][..., 1:][..., ::-1], ai[1][..., 1:][..., ::-1])
    wc = jnp.concatenate([f_re, b_re_, f_im, b_im_], axis=1).reshape(G, 4 * P, C * Q)

    aqr, aqi = ar[..., Q], ai[..., Q]
    coef = jnp.stack([jnp.concatenate([aqr[0], aqr[1]], axis=-1),
                      jnp.concatenate([aqi[0], aqi[1]], axis=-1)], axis=1)
    coef = jnp.concatenate([coef, jnp.zeros((G, 6, 2 * P), f32)], axis=1)
    return kfull, wb.astype(bf16), wc.astype(bf16), coef


def _s5_kernel(nb, nc, ut_ref, kf_ref, wb_ref, wc_ref, coef_ref, dsk_ref, o_ref,
               x_ref, t_ref, cst_ref, hin_ref):
    C = SSM_GROUP
    P2 = 2 * SSM_STATE
    for c in range(C):
        x_ref[:, c * Q:(c + 1) * Q] = ut_ref[c]
    x = x_ref[...]

    def build(cp, carry):
        r0 = pl.multiple_of(cp * Q, Q)
        for c in range(C):
            row = kf_ref[0, cp, pl.ds(c, 1), :]
            blk = jnp.broadcast_to(row, (Q, 2 * Q))
            rolled = pltpu.roll(blk, 0, 1, stride=1, stride_axis=0)
            t_ref[pl.ds(r0, Q), c * Q:(c + 1) * Q] = rolled[:, :Q].astype(bf16)
        return carry
    lax.fori_loop(0, C, build, 0)

    cst_ref[...] = jnp.dot(x, wb_ref[0], preferred_element_type=f32)
    P = SSM_STATE
    aqr = coef_ref[0, 0:1, :]
    aqi = coef_ref[0, 1:2, :]
    is_fwd = lax.broadcasted_iota(jnp.int32, (1, P2), 1) < P
    for b in range(nb):
        hr = jnp.zeros((1, P2), f32)
        hi = jnp.zeros((1, P2), f32)
        for n in range(nc):
            rf = b * nc + n
            rb = b * nc + (nc - 1 - n)
            hin_ref[rf:rf + 1, 0:P] = hr[:, 0:P]
            hin_ref[rb:rb + 1, P:P2] = hr[:, P:P2]
            hin_ref[rf:rf + 1, P2:P2 + P] = hi[:, 0:P]
            hin_ref[rb:rb + 1, P2 + P:2 * P2] = hi[:, P:P2]
            cr = jnp.where(is_fwd, cst_ref[rf:rf + 1, 0:P2], cst_ref[rb:rb + 1, 0:P2])
            ci = jnp.where(is_fwd, cst_ref[rf:rf + 1, P2:2 * P2], cst_ref[rb:rb + 1, P2:2 * P2])
            hr, hi = aqr * hr - aqi * hi + cr, aqr * hi + aqi * hr + ci

    hin = hin_ref[...].astype(bf16)
    for cp in range(C // 2):
        cols = slice(cp * 2 * Q, (cp + 1) * 2 * Q)
        y2 = (jnp.dot(x, t_ref[:, cols], preferred_element_type=f32)
              + jnp.dot(hin, wc_ref[0, :, cols], preferred_element_type=f32))
        for h in range(2):
            c = 2 * cp + h
            y = y2[:, h * Q:(h + 1) * Q] + dsk_ref[0, c:c + 1, :] * ut_ref[c].astype(f32)
            o_ref[c] = jax.nn.gelu(y)


def _s5(ut, nb, nc, kfull, wb, wc, coef, d_skip):
    rows = nb * nc
    ut3 = ut.reshape(SSM_WIDTH, rows, Q)
    dsk = jnp.broadcast_to(d_skip.astype(f32).reshape(SSM_GROUPS, SSM_GROUP, 1), (SSM_GROUPS, SSM_GROUP, Q))
    C = SSM_GROUP
    out = pl.pallas_call(
        functools.partial(_s5_kernel, nb, nc),
        out_shape=jax.ShapeDtypeStruct((SSM_WIDTH, rows, Q), f32),
        grid=(SSM_GROUPS,),
        in_specs=[
            pl.BlockSpec((C, rows, Q), lambda g: (g, 0, 0)),
            pl.BlockSpec((1, C, C, 2 * Q), lambda g: (g, 0, 0, 0)),
            pl.BlockSpec((1, GW, 4 * SSM_STATE), lambda g: (g, 0, 0)),
            pl.BlockSpec((1, 4 * SSM_STATE, GW), lambda g: (g, 0, 0)),
            pl.BlockSpec((1, 8, 2 * SSM_STATE), lambda g: (g, 0, 0)),
            pl.BlockSpec((1, C, Q), lambda g: (g, 0, 0)),
        ],
        out_specs=pl.BlockSpec((C, rows, Q), lambda g: (g, 0, 0)),
        scratch_shapes=[
            pltpu.VMEM((rows, GW), bf16),
            pltpu.VMEM((GW, GW), bf16),
            pltpu.VMEM((rows, 4 * SSM_STATE), f32),
            pltpu.VMEM((rows, 4 * SSM_STATE), f32),
        ],
        compiler_params=_cparams(("parallel",), VMEM_LIMIT),
        name="s5_mixer",
    )(ut3, kfull, wb, wc, coef, dsk)
    return out.reshape(SSM_WIDTH, rows * Q)


def _alibi_slopes():
    return [float(2.0 ** (-8.0 * (h + 1) / N_HEADS)) for h in range(N_HEADS)]


def _attn_kernel(q_ref, km_ref, kp_ref, ko_ref, kn_ref, sink_ref, o_ref):
    jq = pl.program_id(1)
    nq = pl.num_programs(1)
    kv = jnp.concatenate([km_ref[...], kp_ref[...], ko_ref[...], kn_ref[...]], axis=0)
    nk = 4 * BLOCK
    row = lax.broadcasted_iota(jnp.int32, (BLOCK, nk), 0)
    col = lax.broadcasted_iota(jnp.int32, (BLOCK, nk), 1)
    seg = col // BLOCK
    cc = col % BLOCK
    rel = (seg - 2) * BLOCK + cc - row
    dist = jnp.abs(rel)
    band = (seg >= 1) & (dist <= WINDOW)
    band = band & jnp.logical_not((seg == 1) & (jq == 0))
    band = band & jnp.logical_not((seg == 3) & (jq == nq - 1))
    meta = (seg == 0) & (cc >= BLOCK - N_META)
    distf = dist.astype(f32)
    slopes = _alibi_slopes()
    grp = N_HEADS // N_KV_HEADS
    for h in range(N_HEADS):
        g = h // grp
        qh = q_ref[:, h * HEAD_DIM:(h + 1) * HEAD_DIM]
        kh = kv[:, g * HEAD_DIM:(g + 1) * HEAD_DIM]
        vh = kv[:, KV_WIDTH + g * HEAD_DIM:KV_WIDTH + (g + 1) * HEAD_DIM]
        s = _dot_nt(qh, kh)
        s = jnp.where(band, s - slopes[h] * distf, jnp.where(meta, s, NEG_INF))
        sink = sink_ref[h]
        m = jnp.maximum(jnp.max(s, axis=1, keepdims=True), sink)
        p = jnp.exp(s - m)
        den = jnp.sum(p, axis=1, keepdims=True) + jnp.exp(sink - m)
        o = jnp.dot(p.astype(bf16), vh, preferred_element_type=f32) / den
        o_ref[:, h * HEAD_DIM:(h + 1) * HEAD_DIM] = o.astype(bf16)


def _attention(qs, kvs, sink, nb, nq):
    pb = PAD_FRONT // BLOCK
    nbp = nq + pb
    kv_spec = lambda f: pl.BlockSpec((BLOCK, 2 * KV_WIDTH), f)
    return pl.pallas_call(
        _attn_kernel,
        out_shape=jax.ShapeDtypeStruct((nb * nq * BLOCK, ATTN_WIDTH), bf16),
        grid=(nb, nq),
        in_specs=[
            pl.BlockSpec((BLOCK, ATTN_WIDTH), lambda b, j: (b * nbp + pb + j, 0)),
            kv_spec(lambda b, j: (b * nbp + pb - 1, 0)),
            kv_spec(lambda b, j: (b * nbp + pb + j - 1, 0)),
            kv_spec(lambda b, j: (b * nbp + pb + j, 0)),
            kv_spec(lambda b, j: (b * nbp + jnp.minimum(pb + j + 1, nbp - 1), 0)),
            pl.BlockSpec(memory_space=pltpu.SMEM),
        ],
        out_specs=pl.BlockSpec((BLOCK, ATTN_WIDTH), lambda b, j: (b * nq + j, 0)),
        compiler_params=_cparams(("parallel", "parallel")),
        name="window_attn",
    )(qs, kvs, kvs, kvs, kvs, sink.astype(f32))


MG_TM = 256


def _merge_kernel(yt_ref, yb_ref, g0_ref, g1_ref, x_ref, wglu_ref, bglu_ref, wps_ref, wpa_ref, wout_ref, o_ref):
    y = yt_ref[...].T
    gl = jnp.dot(y.astype(bf16), wglu_ref[...], preferred_element_type=f32) + bglu_ref[...]
    ya = y * jax.nn.sigmoid(gl)
    pa = jnp.dot(ya.astype(bf16), wps_ref[...], preferred_element_type=f32)
    pbv = jnp.dot(yb_ref[...], wpa_ref[...], preferred_element_type=f32)
    merged = (g0_ref[...].astype(f32) * pa + g1_ref[...].astype(f32) * pbv).astype(bf16)
    o_ref[0] = x_ref[0] + jnp.dot(merged, wout_ref[...], preferred_element_type=f32)


def _merge(yt, yb, gates, x, w_glu, b_glu, w_ps, w_pa, w_out):
    nb, seq, d = x.shape
    tm = MG_TM
    nt = seq // tm
    lp_t = (seq + PAD_FRONT) // tm
    off = PAD_FRONT // tm
    return pl.pallas_call(
        _merge_kernel,
        out_shape=jax.ShapeDtypeStruct((nb, seq, d), f32),
        grid=(nb, nt),
        in_specs=[
            pl.BlockSpec((SSM_WIDTH, tm), lambda b, i: (0, b * lp_t + off + i)),
            pl.BlockSpec((tm, ATTN_WIDTH), lambda b, i: (b * nt + i, 0)),
            pl.BlockSpec((tm, d), lambda b, i: (b * nt + i, 0)),
            pl.BlockSpec((tm, d), lambda b, i: (b * nt + i, 1)),
            pl.BlockSpec((1, tm, d), lambda b, i: (b, i, 0)),
            _resident((SSM_WIDTH, SSM_WIDTH)),
            _resident((1, SSM_WIDTH)),
            _resident((SSM_WIDTH, d)),
            _resident((ATTN_WIDTH, d)),
            _resident((d, d)),
        ],
        out_specs=pl.BlockSpec((1, tm, d), lambda b, i: (b, i, 0)),
        compiler_params=_cparams(("parallel", "parallel"), VMEM_LIMIT),
        name="merge_out",
    )(yt, yb, gates, gates, x, w_glu.astype(bf16), b_glu.astype(f32)[None], w_ps.astype(bf16), w_pa.astype(bf16),
      w_out.astype(bf16))


RT_TM = 256


def _topk_rows(s, order, payload, vals_ref, pay_ref, r0):
    big = jnp.int32(2 ** 30)
    for i in range(PEER_TOPK):
        m = jnp.max(s, axis=0, keepdims=True)
        idx = jnp.min(jnp.where(s == m, order, big), axis=0, keepdims=True)
        sel = order == idx
        vals_ref[i:i + 1, :] = m
        if payload is None:
            pay_ref[r0 + i:r0 + i + 1, :] = idx
        else:
            pay_ref[r0 + i:r0 + i + 1, :] = jnp.max(jnp.where(sel, payload, -1), axis=0, keepdims=True)
        s = jnp.where(sel, -jnp.inf, s)


_CAND_BLOCKS = (
    ("a", 0, 0, 16, 0), ("a", 1, 0, 8, 0), ("a", 2, 0, 8, 0), ("a", 3, 0, 8, 0),
    ("b", 0, 0, 8, 4), ("b", 1, 0, 8, 4), ("b", 2, 0, 8, 4), ("b", 0, 8, 8, 0),
)
_CAND_ROWS = sum(blk[3] for blk in _CAND_BLOCKS)


def _route_kernel(hs_ref, g_ref, wq_ref, k1_ref, k2_ref, eidx_ref, gate_ref,
                  v1_ref, i1_ref, v2_ref, i2_ref, cand_ref, cidx_ref, sc_ref):
    h2 = _rms(hs_ref[...], g_ref[...])
    qf = jnp.dot(h2.astype(bf16), wq_ref[...], preferred_element_type=f32).astype(bf16)
    K = PEER_TOPK
    tm = qf.shape[0]
    keyid = lax.broadcasted_iota(jnp.int32, (PEER_KEYS, tm), 0)
    flat, dup = [], []
    for axis, fixed, first, n, drop in _CAND_BLOCKS:
        var = lax.broadcasted_iota(jnp.int32, (n, tm), 0) + first
        flat.append(fixed * K + var if axis == "a" else var * K + fixed)
        dup.append(var < drop)
    flat = jnp.concatenate(flat, axis=0)
    dup = jnp.concatenate(dup, axis=0)
    for h in range(PEER_HEADS):
        q1 = qf[:, h * 2 * PEER_HALF:h * 2 * PEER_HALF + PEER_HALF]
        q2 = qf[:, h * 2 * PEER_HALF + PEER_HALF:(h + 1) * 2 * PEER_HALF]
        s1 = _dot_nt(k1_ref[h], q1)
        s2 = _dot_nt(k2_ref[h], q2)
        _topk_rows(s1, keyid, None, v1_ref, i1_ref, 0)
        _topk_rows(s2, keyid, None, v2_ref, i2_ref, 0)
        r = 0
        for axis, fixed, first, n, drop in _CAND_BLOCKS:
            if axis == "a":
                va, ia = v1_ref[fixed:fixed + 1, :], i1_ref[fixed:fixed + 1, :]
                vb, ib = v2_ref[first:first + n, :], i2_ref[first:first + n, :]
            else:
                va, ia = v1_ref[first:first + n, :], i1_ref[first:first + n, :]
                vb, ib = v2_ref[fixed:fixed + 1, :], i2_ref[fixed:fixed + 1, :]
            cand_ref[r:r + n, :] = va + vb
            cidx_ref[r:r + n, :] = ia * PEER_KEYS + ib
            r += n
        cand = jnp.where(dup, -jnp.inf, cand_ref[...])
        _topk_rows(cand, flat, cidx_ref[...], sc_ref, eidx_ref, h * K)
        sc = sc_ref[...]
        e = jnp.exp(sc - sc[0:1, :])
        gate_ref[h * K:(h + 1) * K, :] = e / jnp.sum(e, axis=0, keepdims=True)


def _route(hs2, norm_g, w_q, k1, k2):
    rows = hs2.shape[0]
    tm = RT_TM
    K = PEER_TOPK
    return pl.pallas_call(
        _route_kernel,
        out_shape=(
            jax.ShapeDtypeStruct((PEER_SLOTS, rows), jnp.int32),
            jax.ShapeDtypeStruct((PEER_SLOTS, rows), f32),
        ),
        grid=(rows // tm,),
        in_specs=[
            pl.BlockSpec((tm, D_MODEL), lambda i: (i, 0)),
            _resident((1, D_MODEL)),
            _resident((D_MODEL, PEER_HEADS * 2 * PEER_HALF)),
            _resident((PEER_HEADS, PEER_KEYS, PEER_HALF)),
            _resident((PEER_HEADS, PEER_KEYS, PEER_HALF)),
        ],
        out_specs=(
            pl.BlockSpec((PEER_SLOTS, tm), lambda i: (0, i)),
            pl.BlockSpec((PEER_SLOTS, tm), lambda i: (0, i)),
        ),
        scratch_shapes=[
            pltpu.VMEM((K, tm), f32), pltpu.VMEM((K, tm), jnp.int32),
            pltpu.VMEM((K, tm), f32), pltpu.VMEM((K, tm), jnp.int32),
            pltpu.VMEM((_CAND_ROWS, tm), f32), pltpu.VMEM((_CAND_ROWS, tm), jnp.int32),
            pltpu.VMEM((K, tm), f32),
        ],
        compiler_params=_cparams(("parallel",), VMEM_LIMIT),
        name="peer_route",
    )(hs2, norm_g.astype(f32)[None], w_q.astype(bf16), k1.astype(bf16), k2.astype(bf16))


GT_LC = D_MODEL // 128


def _pack_kernel(u_ref, v_ref, o_ref):
    ub = pltpu.bitcast(u_ref[...].astype(bf16).astype(f32), jnp.uint32)
    vb = pltpu.bitcast(v_ref[...].astype(bf16).astype(f32), jnp.uint32)
    w = ub | (vb >> 16)
    for c in range(GT_LC):
        o_ref[:, c, :, :] = w[:, c * 128:(c + 1) * 128].reshape(w.shape[0], 1, 128)


def _pack_tables(u_emb, v_emb):
    n = u_emb.shape[0]
    tm = 256
    spec = pl.BlockSpec((tm, D_MODEL), lambda i: (i, 0))
    return pl.pallas_call(
        _pack_kernel,
        out_shape=jax.ShapeDtypeStruct((n, GT_LC, 1, 128), jnp.uint32),
        grid=(n // tm,),
        in_specs=[spec, spec],
        out_specs=pl.BlockSpec((tm, GT_LC, 1, 128), lambda i: (i, 0, 0, 0)),
        compiler_params=_cparams(("parallel",)),
        name="peer_pack",
    )(u_emb, v_emb)


GT_TQ = 8
GT_ROWS = GT_TQ * PEER_SLOTS


GT_SUB = 8
GT_RG = PEER_SLOTS // GT_SUB


def _row_copy(tbl_ref, buf_ref, sem, e, rg, s):
    return pltpu.make_async_copy(tbl_ref.at[e], buf_ref.at[rg, :, pl.ds(s, 1), :], sem)


def _expert_kernel(idx_ref, idxn_ref, gate_ref, hs_ref, gn_ref, gf_ref, tbl_ref, o_ref,
                   buf0_ref, buf1_ref, sem_ref, acc_ref):
    i = pl.program_id(0)
    nt = pl.num_programs(0)

    @pl.when(i == 0)
    def _():
        def body(j, carry):
            _row_copy(tbl_ref, buf0_ref, sem_ref.at[0], idx_ref[0, 0, j], j >> 3, j & 7).start()
            return carry
        lax.fori_loop(0, GT_ROWS, body, 0, unroll=8)

    def wait_all(buf_ref, sem):
        pltpu.make_async_copy(buf_ref, buf_ref, sem).wait()

    def step(cur_ref, cur_sem, nxt_ref, nxt_sem):
        wait_all(cur_ref, cur_sem)
        gate = gate_ref[0]
        h2 = _rms(hs_ref[...], gn_ref[...])
        hi_mask = jnp.uint32(0xFFFF0000)
        for t in range(GT_TQ):
            for k in range(PEER_SLOTS):
                j = t * PEER_SLOTS + k
                _row_copy(tbl_ref, nxt_ref, nxt_sem, idxn_ref[0, 0, j], j // GT_SUB, j % GT_SUB).start(priority=k % 2)
            hb = [jnp.broadcast_to(h2[t:t + 1, c * 128:(c + 1) * 128], (GT_SUB, 128)) for c in range(GT_LC)]
            parts = []
            for g in range(GT_RG):
                p = None
                for c in range(GT_LC):
                    u = pltpu.bitcast(cur_ref[t * GT_RG + g, c] & hi_mask, f32) * hb[c]
                    p = u if p is None else p + u
                parts.append(p)
            a = jnp.sum(jnp.concatenate(parts, axis=0), axis=1, keepdims=True)
            w = jnp.broadcast_to(gate[:, t:t + 1] * jax.nn.gelu(a), (PEER_SLOTS, 128))
            for c in range(GT_LC):
                o = None
                for g in range(GT_RG):
                    v = pltpu.bitcast(cur_ref[t * GT_RG + g, c] << 16, f32) * w[g * GT_SUB:(g + 1) * GT_SUB]
                    o = v if o is None else o + v
                acc_ref[t:t + 1, c * 128:(c + 1) * 128] = jnp.sum(o, axis=0, keepdims=True)
        o_ref[...] = _rms(hs_ref[...] + acc_ref[...], gf_ref[...])

        @pl.when(i == nt - 1)
        def _():
            wait_all(nxt_ref, nxt_sem)

    @pl.when(i % 2 == 0)
    def _():
        step(buf0_ref, sem_ref.at[0], buf1_ref, sem_ref.at[1])

    @pl.when(i % 2 == 1)
    def _():
        step(buf1_ref, sem_ref.at[1], buf0_ref, sem_ref.at[0])


def _experts(eidx, gate, hs2, norm_ffn, norm_g, tbl):
    rows = hs2.shape[0]
    nt = rows // GT_TQ
    idx3 = eidx.T.reshape(nt, 1, GT_ROWS)
    gate3 = jnp.transpose(gate.reshape(PEER_SLOTS, nt, GT_TQ), (1, 0, 2))
    row_spec = pl.BlockSpec((GT_TQ, D_MODEL), lambda i: (i, 0))
    buf = pltpu.VMEM((GT_ROWS // GT_SUB, GT_LC, GT_SUB, 128), jnp.uint32)
    return pl.pallas_call(
        _expert_kernel,
        out_shape=jax.ShapeDtypeStruct((rows, D_MODEL), f32),
        grid=(nt,),
        in_specs=[
            pl.BlockSpec((1, 1, GT_ROWS), lambda i: (i, 0, 0), memory_space=pltpu.SMEM),
            pl.BlockSpec((1, 1, GT_ROWS), lambda i: (jnp.minimum(i + 1, nt - 1), 0, 0), memory_space=pltpu.SMEM),
            pl.BlockSpec((1, PEER_SLOTS, GT_TQ), lambda i: (i, 0, 0)),
            row_spec,
            pl.BlockSpec((1, D_MODEL), lambda i: (0, 0)),
            pl.BlockSpec((1, D_MODEL), lambda i: (0, 0)),
            pl.BlockSpec(memory_space=pl.ANY),
        ],
        out_specs=row_spec,
        scratch_shapes=[
            buf,
            buf,
            pltpu.SemaphoreType.DMA((2,)),
            pltpu.VMEM((GT_TQ, D_MODEL), f32),
        ],
        compiler_params=_cparams(("arbitrary",), VMEM_LIMIT),
        name="peer_experts",
    )(idx3, idx3, gate3, hs2, norm_ffn.astype(f32)[None], norm_g.astype(f32)[None], tbl)


def kernel(x, meta_tokens, norm_mix, w_in, b_gate, ssm_lam_re, ssm_lam_im, ssm_log_dt, ssm_b_re, ssm_b_im, ssm_c_re, ssm_c_im, ssm_d, ssm_w_glu, ssm_b_glu, attn_sink, w_proj_ssm, w_proj_attn, w_out, norm_ffn, peer_w_q, peer_k1, peer_k2, peer_u, peer_v, norm_final):
    nb, seq, d = x.shape
    assert w_in.shape[0] == 1 and d == D_MODEL and seq % GI_TM == 0
    l = 0
    lp = seq + PAD_FRONT
    nc = lp // Q

    front = jnp.concatenate([jnp.zeros((PAD_FRONT - N_META, d), x.dtype), meta_tokens.astype(x.dtype)], axis=0)
    ut, qs, kvs = _mixin(x, front, norm_mix[l], w_in[l])
    gates = _gatein(x.reshape(nb * seq, d), norm_mix[l], w_in[l], b_gate[l])

    kfull, wb, wc, coef = _s5_weights(ssm_lam_re[l], ssm_lam_im[l], ssm_log_dt[l], ssm_b_re[l], ssm_b_im[l],
                                      ssm_c_re[l], ssm_c_im[l])
    yt = _s5(ut, nb, nc, kfull, wb, wc, coef, ssm_d[l])

    yb = _attention(qs, kvs, attn_sink[l], nb, seq // BLOCK)

    hs2 = _merge(yt, yb, gates, x, ssm_w_glu[l], ssm_b_glu[l], w_proj_ssm[l], w_proj_attn[l], w_out[l])
    hs2 = hs2.reshape(nb * seq, d)

    eidx, gate = _route(hs2, norm_ffn[l], peer_w_q[l], peer_k1[l], peer_k2[l])
    tbl = _pack_tables(peer_u[l], peer_v[l])
    out = _experts(eidx, gate, hs2, norm_ffn[l], norm_final, tbl)
    return out.reshape(nb, seq, d)
```

```python
import functools
import math

import numpy as np
import jax
import jax.numpy as jnp
from jax import lax
from jax.experimental import pallas as pl
from jax.experimental.pallas import tpu as pltpu

f32 = jnp.float32
bf16 = jnp.bfloat16

D_MODEL = 2048
N_META = 16
SSM_WIDTH = 1024
SSM_GROUP = 16
SSM_GROUPS = 64
SSM_STATE = 64
HEAD_DIM = 128
N_HEADS = 8
N_KV_HEADS = 2
ATTN_WIDTH = 1024
KV_WIDTH = 256
WINDOW = 128
BLOCK = 128
PEER_HEADS = 8
PEER_KEYS = 128
PEER_HALF = 128
PEER_TOPK = 16
PEER_SLOTS = PEER_HEADS * PEER_TOPK
NORM_EPS = 1e-6
NEG_INF = -1e30

PAD_FRONT = 256
VMEM_LIMIT = 56 * 1024 * 1024

HI = lax.Precision.HIGHEST


def _cparams(sem, vmem=None):
    return pltpu.CompilerParams(dimension_semantics=sem, vmem_limit_bytes=vmem)


def _rms(x, g):
    ms = jnp.mean(x * x, axis=-1, keepdims=True)
    return x * lax.rsqrt(ms + NORM_EPS) * g


def _dot_nt(a, b):
    return lax.dot_general(a, b, (((1,), (1,)), ((), ())), preferred_element_type=f32)


def _resident(shape):
    return pl.BlockSpec(shape, lambda *_: (0,) * len(shape), pipeline_mode=pl.Buffered(1))


MX_TM = PAD_FRONT


def _mixin_kernel(x_ref, front_ref, g_ref, wut_ref, wqkv_ref, ut_ref, q_ref, kv_ref):
    xin = jnp.where(pl.program_id(1) == 0, front_ref[...], x_ref[0])
    xn = _rms(xin, g_ref[...]).astype(bf16)
    ut_ref[...] = _dot_nt(wut_ref[...], xn).astype(bf16)
    z = jnp.dot(xn, wqkv_ref[...], preferred_element_type=f32)
    q_ref[...] = (z[:, :ATTN_WIDTH] * (HEAD_DIM ** -0.5)).astype(bf16)
    kv_ref[...] = z[:, ATTN_WIDTH:].astype(bf16)


def _mixin(x, front, norm_g, w_in):
    nb, seq, d = x.shape
    nt = seq // MX_TM + 1
    rows = nb * nt * MX_TM
    wut = w_in[:, :SSM_WIDTH].T.astype(bf16)
    wqkv = w_in[:, SSM_WIDTH:SSM_WIDTH + ATTN_WIDTH + 2 * KV_WIDTH].astype(bf16)
    return pl.pallas_call(
        _mixin_kernel,
        out_shape=(
            jax.ShapeDtypeStruct((SSM_WIDTH, rows), bf16),
            jax.ShapeDtypeStruct((rows, ATTN_WIDTH), bf16),
            jax.ShapeDtypeStruct((rows, 2 * KV_WIDTH), bf16),
        ),
        grid=(nb, nt),
        in_specs=[
            pl.BlockSpec((1, MX_TM, d), lambda b, i: (b, jnp.maximum(i - 1, 0), 0)),
            _resident((MX_TM, d)),
            _resident((1, d)),
            _resident((SSM_WIDTH, d)),
            _resident((d, ATTN_WIDTH + 2 * KV_WIDTH)),
        ],
        out_specs=(
            pl.BlockSpec((SSM_WIDTH, MX_TM), lambda b, i: (0, b * nt + i)),
            pl.BlockSpec((MX_TM, ATTN_WIDTH), lambda b, i: (b * nt + i, 0)),
            pl.BlockSpec((MX_TM, 2 * KV_WIDTH), lambda b, i: (b * nt + i, 0)),
        ),
        compiler_params=_cparams(("parallel", "parallel"), VMEM_LIMIT),
        name="mixer_in",
    )(x, front, norm_g.astype(f32)[None], wut, wqkv)


GI_TM = 512
GI_TN = 2048


def _gatein_kernel(x_ref, g_ref, w_ref, b_ref, o_ref, xn_ref):
    @pl.when(pl.program_id(1) == 0)
    def _():
        xn_ref[...] = _rms(x_ref[...], g_ref[...]).astype(bf16)

    z = jnp.dot(xn_ref[...], w_ref[...], preferred_element_type=f32)
    o_ref[...] = jax.nn.sigmoid(z + b_ref[...]).astype(bf16)


def _gatein(x2d, norm_g, w_in, b_gate):
    rows, d = x2d.shape
    wg = w_in[:, SSM_WIDTH + ATTN_WIDTH + 2 * KV_WIDTH:].astype(bf16)
    ng = wg.shape[1]
    return pl.pallas_call(
        _gatein_kernel,
        out_shape=jax.ShapeDtypeStruct((rows, ng), bf16),
        grid=(rows // GI_TM, ng // GI_TN),
        in_specs=[
            pl.BlockSpec((GI_TM, d), lambda i, j: (i, 0)),
            _resident((1, d)),
            pl.BlockSpec((d, GI_TN), lambda i, j: (0, j)),
            pl.BlockSpec((1, GI_TN), lambda i, j: (0, j)),
        ],
        out_specs=pl.BlockSpec((GI_TM, GI_TN), lambda i, j: (i, j)),
        scratch_shapes=[pltpu.VMEM((GI_TM, d), bf16)],
        compiler_params=_cparams(("parallel", "arbitrary"), VMEM_LIMIT),
        name="gate_in",
    )(x2d, norm_g.astype(f32)[None], wg, b_gate.astype(f32)[None])


Q = BLOCK
GW = SSM_GROUP * Q


def _s5_weights(lam_re, lam_im, log_dt, b_re, b_im, c_re, c_im):
    dt = jnp.exp(log_dt.astype(f32))[..., None]
    lr = lam_re.astype(f32)
    li = lam_im.astype(f32)
    k = jnp.arange(Q + 1, dtype=f32)
    mag = jnp.exp(lr[..., None] * dt[..., None] * k)
    ang = li[..., None] * dt[..., None] * k
    ar = mag * jnp.cos(ang)
    ai = mag * jnp.sin(ang)
    a1r, a1i = ar[..., 1], ai[..., 1]
    den = lr * lr + li * li
    nr, ni = a1r - 1.0, a1i
    cr = (nr * lr + ni * li) / den
    ci = (ni * lr - nr * li) / den
    br, bi = b_re.astype(f32), b_im.astype(f32)
    bbr = cr[..., None] * br - ci[..., None] * bi
    bbi = cr[..., None] * bi + ci[..., None] * br
    ccr = jnp.swapaxes(c_re.astype(f32), -1, -2)
    cci = jnp.swapaxes(c_im.astype(f32), -1, -2)

    cbr = ccr[..., :, None] * bbr[..., None, :] - cci[..., :, None] * bbi[..., None, :]
    cbi = ccr[..., :, None] * bbi[..., None, :] + cci[..., :, None] * bbr[..., None, :]
    G, P, C = cbr.shape[1], cbr.shape[2], cbr.shape[3]
    cbr2 = cbr.reshape(2, G, P, C * C)
    cbi2 = cbi.reshape(2, G, P, C * C)
    kk = (jnp.einsum('dgpk,dgpx->dgkx', ar[..., :Q], cbr2, precision=HI)
          - jnp.einsum('dgpk,dgpx->dgkx', ai[..., :Q], cbi2, precision=HI))
    kk = kk.reshape(2, G, Q, C, C)
    kf, kb = kk[0], kk[1]
    lag0 = (kf[:, 0] + kb[:, 0])[:, None]
    zero = jnp.zeros_like(lag0)
    kfull = jnp.concatenate([lag0, kf[:, 1:], zero, kb[:, 1:][:, ::-1]], axis=1)
    kfull = jnp.transpose(kfull, (0, 3, 2, 1))

    def bu_pow(d, pw_r, pw_i):
        re = pw_r[:, :, None, :] * bbr[d][..., None] - pw_i[:, :, None, :] * bbi[d][..., None]
        im = pw_r[:, :, None, :] * bbi[d][..., None] + pw_i[:, :, None, :] * bbr[d][..., None]
        return jnp.transpose(re, (0, 2, 3, 1)), jnp.transpose(im, (0, 2, 3, 1))
    fr, fi = bu_pow(0, ar[0][..., :Q][..., ::-1], ai[0][..., :Q][..., ::-1])
    rr, ri = bu_pow(1, ar[1][..., :Q], ai[1][..., :Q])
    wb = jnp.concatenate([fr, rr, fi, ri], axis=-1).reshape(G, C * Q, 4 * P)

    def c_pow(d, pw_r, pw_i):
        re = ccr[d][..., None] * pw_r[:, :, None, :] - cci[d][..., None] * pw_i[:, :, None, :]
        im = ccr[d][..., None] * pw_i[:, :, None, :] + cci[d][..., None] * pw_r[:, :, None, :]
        return re, -im
    f_re, f_im = c_pow(0, ar[0][..., 1:], ai[0][..., 1:])
    b_re_, b_im_ = c_pow(1, ar[1][..., 1:][..., ::-1], ai[1][..., 1:][..., ::-1])
    wc = jnp.concatenate([f_re, b_re_, f_im, b_im_], axis=1).reshape(G, 4 * P, C * Q)

    aqr, aqi = ar[..., Q], ai[..., Q]
    coef = jnp.stack([jnp.concatenate([aqr[0], aqr[1]], axis=-1),
                      jnp.concatenate([aqi[0], aqi[1]], axis=-1)], axis=1)
    coef = jnp.concatenate([coef, jnp.zeros((G, 6, 2 * P), f32)], axis=1)
    return kfull, wb.astype(bf16), wc.astype(bf16), coef


def _s5_kernel(nb, nc, ut_ref, kf_ref, wb_ref, wc_ref, coef_ref, dsk_ref, o_ref,
               x_ref, t_ref, cst_ref, hin_ref):
    C = SSM_GROUP
    P2 = 2 * SSM_STATE
    for c in range(C):
        x_ref[:, c * Q:(c + 1) * Q] = ut_ref[c]
    x = x_ref[...]

    def build(cp, carry):
        r0 = pl.multiple_of(cp * Q, Q)
        for c in range(C):
            row = kf_ref[0, cp, pl.ds(c, 1), :]
            blk = jnp.broadcast_to(row, (Q, 2 * Q))
            rolled = pltpu.roll(blk, 0, 1, stride=1, stride_axis=0)
            t_ref[pl.ds(r0, Q), c * Q:(c + 1) * Q] = rolled[:, :Q].astype(bf16)
        return carry
    lax.fori_loop(0, C, build, 0)

    cst_ref[...] = jnp.dot(x, wb_ref[0], preferred_element_type=f32)
    P = SSM_STATE
    aqr = coef_ref[0, 0:1, :]
    aqi = coef_ref[0, 1:2, :]
    is_fwd = lax.broadcasted_iota(jnp.int32, (1, P2), 1) < P
    for b in range(nb):
        hr = jnp.zeros((1, P2), f32)
        hi = jnp.zeros((1, P2), f32)
        for n in range(nc):
            rf = b * nc + n
            rb = b * nc + (nc - 1 - n)
            hin_ref[rf:rf + 1, 0:P] = hr[:, 0:P]
            hin_ref[rb:rb + 1, P:P2] = hr[:, P:P2]
            hin_ref[rf:rf + 1, P2:P2 + P] = hi[:, 0:P]
            hin_ref[rb:rb + 1, P2 + P:2 * P2] = hi[:, P:P2]
            cr = jnp.where(is_fwd, cst_ref[rf:rf + 1, 0:P2], cst_ref[rb:rb + 1, 0:P2])
            ci = jnp.where(is_fwd, cst_ref[rf:rf + 1, P2:2 * P2], cst_ref[rb:rb + 1, P2:2 * P2])
            hr, hi = aqr * hr - aqi * hi + cr, aqr * hi + aqi * hr + ci

    hin = hin_ref[...].astype(bf16)
    for cp in range(C // 2):
        cols = slice(cp * 2 * Q, (cp + 1) * 2 * Q)
        y2 = (jnp.dot(x, t_ref[:, cols], preferred_element_type=f32)
              + jnp.dot(hin, wc_ref[0, :, cols], preferred_element_type=f32))
        for h in range(2):
            c = 2 * cp + h
            y = y2[:, h * Q:(h + 1) * Q] + dsk_ref[0, c:c + 1, :] * ut_ref[c].astype(f32)
            o_ref[c] = jax.nn.gelu(y)


def _s5(ut, nb, nc, kfull, wb, wc, coef, d_skip):
    rows = nb * nc
    ut3 = ut.reshape(SSM_WIDTH, rows, Q)
    dsk = jnp.broadcast_to(d_skip.astype(f32).reshape(SSM_GROUPS, SSM_GROUP, 1), (SSM_GROUPS, SSM_GROUP, Q))
    C = SSM_GROUP
    out = pl.pallas_call(
        functools.partial(_s5_kernel, nb, nc),
        out_shape=jax.ShapeDtypeStruct((SSM_WIDTH, rows, Q), f32),
        grid=(SSM_GROUPS,),
        in_specs=[
            pl.BlockSpec((C, rows, Q), lambda g: (g, 0, 0)),
            pl.BlockSpec((1, C, C, 2 * Q), lambda g: (g, 0, 0, 0)),
            pl.BlockSpec((1, GW, 4 * SSM_STATE), lambda g: (g, 0, 0)),
            pl.BlockSpec((1, 4 * SSM_STATE, GW), lambda g: (g, 0, 0)),
            pl.BlockSpec((1, 8, 2 * SSM_STATE), lambda g: (g, 0, 0)),
            pl.BlockSpec((1, C, Q), lambda g: (g, 0, 0)),
        ],
        out_specs=pl.BlockSpec((C, rows, Q), lambda g: (g, 0, 0)),
        scratch_shapes=[
            pltpu.VMEM((rows, GW), bf16),
            pltpu.VMEM((GW, GW), bf16),
            pltpu.VMEM((rows, 4 * SSM_STATE), f32),
            pltpu.VMEM((rows, 4 * SSM_STATE), f32),
        ],
        compiler_params=_cparams(("parallel",), VMEM_LIMIT),
        name="s5_mixer",
    )(ut3, kfull, wb, wc, coef, dsk)
    return out.reshape(SSM_WIDTH, rows * Q)


def _alibi_slopes():
    return [float(2.0 ** (-8.0 * (h + 1) / N_HEADS)) for h in range(N_HEADS)]


def _attn_kernel(q_ref, km_ref, kp_ref, ko_ref, kn_ref, sink_ref, o_ref):
    jq = pl.program_id(1)
    nq = pl.num_programs(1)
    kv = jnp.concatenate([km_ref[...], kp_ref[...], ko_ref[...], kn_ref[...]], axis=0)
    nk = 4 * BLOCK
    row = lax.broadcasted_iota(jnp.int32, (BLOCK, nk), 0)
    col = lax.broadcasted_iota(jnp.int32, (BLOCK, nk), 1)
    seg = col // BLOCK
    cc = col % BLOCK
    rel = (seg - 2) * BLOCK + cc - row
    dist = jnp.abs(rel)
    band = (seg >= 1) & (dist <= WINDOW)
    band = band & jnp.logical_not((seg == 1) & (jq == 0))
    band = band & jnp.logical_not((seg == 3) & (jq == nq - 1))
    meta = (seg == 0) & (cc >= BLOCK - N_META)
    distf = dist.astype(f32)
    slopes = _alibi_slopes()
    grp = N_HEADS // N_KV_HEADS
    for h in range(N_HEADS):
        g = h // grp
        qh = q_ref[:, h * HEAD_DIM:(h + 1) * HEAD_DIM]
        kh = kv[:, g * HEAD_DIM:(g + 1) * HEAD_DIM]
        vh = kv[:, KV_WIDTH + g * HEAD_DIM:KV_WIDTH + (g + 1) * HEAD_DIM]
        s = _dot_nt(qh, kh)
        s = jnp.where(band, s - slopes[h] * distf, jnp.where(meta, s, NEG_INF))
        sink = sink_ref[h]
        m = jnp.maximum(jnp.max(s, axis=1, keepdims=True), sink)
        p = jnp.exp(s - m)
        den = jnp.sum(p, axis=1, keepdims=True) + jnp.exp(sink - m)
        o = jnp.dot(p.astype(bf16), vh, preferred_element_type=f32) / den
        o_ref[:, h * HEAD_DIM:(h + 1) * HEAD_DIM] = o.astype(bf16)


def _attention(qs, kvs, sink, nb, nq):
    pb = PAD_FRONT // BLOCK
    nbp = nq + pb
    kv_spec = lambda f: pl.BlockSpec((BLOCK, 2 * KV_WIDTH), f)
    return pl.pallas_call(
        _attn_kernel,
        out_shape=jax.ShapeDtypeStruct((nb * nq * BLOCK, ATTN_WIDTH), bf16),
        grid=(nb, nq),
        in_specs=[
            pl.BlockSpec((BLOCK, ATTN_WIDTH), lambda b, j: (b * nbp + pb + j, 0)),
            kv_spec(lambda b, j: (b * nbp + pb - 1, 0)),
            kv_spec(lambda b, j: (b * nbp + pb + j - 1, 0)),
            kv_spec(lambda b, j: (b * nbp + pb + j, 0)),
            kv_spec(lambda b, j: (b * nbp + jnp.minimum(pb + j + 1, nbp - 1), 0)),
            pl.BlockSpec(memory_space=pltpu.SMEM),
        ],
        out_specs=pl.BlockSpec((BLOCK, ATTN_WIDTH), lambda b, j: (b * nq + j, 0)),
        compiler_params=_cparams(("parallel", "parallel")),
        name="window_attn",
    )(qs, kvs, kvs, kvs, kvs, sink.astype(f32))


MG_TM = 256


def _merge_kernel(yt_ref, yb_ref, g0_ref, g1_ref, x_ref, wglu_ref, bglu_ref, wps_ref, wpa_ref, wout_ref, o_ref):
    y = yt_ref[...].T
    gl = jnp.dot(y.astype(bf16), wglu_ref[...], preferred_element_type=f32) + bglu_ref[...]
    ya = y * jax.nn.sigmoid(gl)
    pa = jnp.dot(ya.astype(bf16), wps_ref[...], preferred_element_type=f32)
    pbv = jnp.dot(yb_ref[...], wpa_ref[...], preferred_element_type=f32)
    merged = (g0_ref[...].astype(f32) * pa + g1_ref[...].astype(f32) * pbv).astype(bf16)
    o_ref[0] = x_ref[0] + jnp.dot(merged, wout_ref[...], preferred_element_type=f32)


def _merge(yt, yb, gates, x, w_glu, b_glu, w_ps, w_pa, w_out):
    nb, seq, d = x.shape
    tm = MG_TM
    nt = seq // tm
    lp_t = (seq + PAD_FRONT) // tm
    off = PAD_FRONT // tm
    return pl.pallas_call(
        _merge_kernel,
        out_shape=jax.ShapeDtypeStruct((nb, seq, d), f32),
        grid=(nb, nt),
        in_specs=[
            pl.BlockSpec((SSM_WIDTH, tm), lambda b, i: (0, b * lp_t + off + i)),
            pl.BlockSpec((tm, ATTN_WIDTH), lambda b, i: (b * nt + i, 0)),
            pl.BlockSpec((tm, d), lambda b, i: (b * nt + i, 0)),
            pl.BlockSpec((tm, d), lambda b, i: (b * nt + i, 1)),
            pl.BlockSpec((1, tm, d), lambda b, i: (b, i, 0)),
            _resident((SSM_WIDTH, SSM_WIDTH)),
            _resident((1, SSM_WIDTH)),
            _resident((SSM_WIDTH, d)),
            _resident((ATTN_WIDTH, d)),
            _resident((d, d)),
        ],
        out_specs=pl.BlockSpec((1, tm, d), lambda b, i: (b, i, 0)),
        compiler_params=_cparams(("parallel", "parallel"), VMEM_LIMIT),
        name="merge_out",
    )(yt, yb, gates, gates, x, w_glu.astype(bf16), b_glu.astype(f32)[None], w_ps.astype(bf16), w_pa.astype(bf16),
      w_out.astype(bf16))


RT_TM = 256


def _topk_rows(s, order, payload, vals_ref, pay_ref, r0):
    big = jnp.int32(2 ** 30)
    for i in range(PEER_TOPK):
        m = jnp.max(s, axis=0, keepdims=True)
        idx = jnp.min(jnp.where(s == m, order, big), axis=0, keepdims=True)
        sel = order == idx
        vals_ref[i:i + 1, :] = m
        if payload is None:
            pay_ref[r0 + i:r0 + i + 1, :] = idx
        else:
            pay_ref[r0 + i:r0 + i + 1, :] = jnp.max(jnp.where(sel, payload, -1), axis=0, keepdims=True)
        s = jnp.where(sel, -jnp.inf, s)


_CAND_BLOCKS = (
    ("a", 0, 0, 16, 0), ("a", 1, 0, 8, 0), ("a", 2, 0, 8, 0), ("a", 3, 0, 8, 0),
    ("b", 0, 0, 8, 4), ("b", 1, 0, 8, 4), ("b", 2, 0, 8, 4), ("b", 0, 8, 8, 0),
)
_CAND_ROWS = sum(blk[3] for blk in _CAND_BLOCKS)


def _route_kernel(hs_ref, g_ref, wq_ref, k1_ref, k2_ref, eidx_ref, gate_ref,
                  v1_ref, i1_ref, v2_ref, i2_ref, cand_ref, cidx_ref, sc_ref):
    _route_tile(hs_ref[...], g_ref, wq_ref, k1_ref, k2_ref, eidx_ref, gate_ref,
                v1_ref, i1_ref, v2_ref, i2_ref, cand_ref, cidx_ref, sc_ref)


def _route_tile(hs, g_ref, wq_ref, k1_ref, k2_ref, eidx_ref, gate_ref,
                v1_ref, i1_ref, v2_ref, i2_ref, cand_ref, cidx_ref, sc_ref):
    h2 = _rms(hs, g_ref[...])
    qf = jnp.dot(h2.astype(bf16), wq_ref[...], preferred_element_type=f32).astype(bf16)
    K = PEER_TOPK
    tm = qf.shape[0]
    keyid = lax.broadcasted_iota(jnp.int32, (PEER_KEYS, tm), 0)
    flat, dup = [], []
    for axis, fixed, first, n, drop in _CAND_BLOCKS:
        var = lax.broadcasted_iota(jnp.int32, (n, tm), 0) + first
        flat.append(fixed * K + var if axis == "a" else var * K + fixed)
        dup.append(var < drop)
    flat = jnp.concatenate(flat, axis=0)
    dup = jnp.concatenate(dup, axis=0)
    for h in range(PEER_HEADS):
        q1 = qf[:, h * 2 * PEER_HALF:h * 2 * PEER_HALF + PEER_HALF]
        q2 = qf[:, h * 2 * PEER_HALF + PEER_HALF:(h + 1) * 2 * PEER_HALF]
        s1 = _dot_nt(k1_ref[h], q1)
        s2 = _dot_nt(k2_ref[h], q2)
        _topk_rows(s1, keyid, None, v1_ref, i1_ref, 0)
        _topk_rows(s2, keyid, None, v2_ref, i2_ref, 0)
        r = 0
        for axis, fixed, first, n, drop in _CAND_BLOCKS:
            if axis == "a":
                va, ia = v1_ref[fixed:fixed + 1, :], i1_ref[fixed:fixed + 1, :]
                vb, ib = v2_ref[first:first + n, :], i2_ref[first:first + n, :]
            else:
                va, ia = v1_ref[first:first + n, :], i1_ref[first:first + n, :]
                vb, ib = v2_ref[fixed:fixed + 1, :], i2_ref[fixed:fixed + 1, :]
            cand_ref[r:r + n, :] = va + vb
            cidx_ref[r:r + n, :] = ia * PEER_KEYS + ib
            r += n
        cand = jnp.where(dup, -jnp.inf, cand_ref[...])
        _topk_rows(cand, flat, cidx_ref[...], sc_ref, eidx_ref, h * K)
        sc = sc_ref[...]
        e = jnp.exp(sc - sc[0:1, :])
        gate_ref[h * K:(h + 1) * K, :] = e / jnp.sum(e, axis=0, keepdims=True)


def _route(hs2, norm_g, w_q, k1, k2):
    rows = hs2.shape[0]
    tm = RT_TM
    K = PEER_TOPK
    return pl.pallas_call(
        _route_kernel,
        out_shape=(
            jax.ShapeDtypeStruct((PEER_SLOTS, rows), jnp.int32),
            jax.ShapeDtypeStruct((PEER_SLOTS, rows), f32),
        ),
        grid=(rows // tm,),
        in_specs=[
            pl.BlockSpec((tm, D_MODEL), lambda i: (i, 0)),
            _resident((1, D_MODEL)),
            _resident((D_MODEL, PEER_HEADS * 2 * PEER_HALF)),
            _resident((PEER_HEADS, PEER_KEYS, PEER_HALF)),
            _resident((PEER_HEADS, PEER_KEYS, PEER_HALF)),
        ],
        out_specs=(
            pl.BlockSpec((PEER_SLOTS, tm), lambda i: (0, i)),
            pl.BlockSpec((PEER_SLOTS, tm), lambda i: (0, i)),
        ),
        scratch_shapes=[
            pltpu.VMEM((K, tm), f32), pltpu.VMEM((K, tm), jnp.int32),
            pltpu.VMEM((K, tm), f32), pltpu.VMEM((K, tm), jnp.int32),
            pltpu.VMEM((_CAND_ROWS, tm), f32), pltpu.VMEM((_CAND_ROWS, tm), jnp.int32),
            pltpu.VMEM((K, tm), f32),
        ],
        compiler_params=_cparams(("parallel",), VMEM_LIMIT),
        name="peer_route",
    )(hs2, norm_g.astype(f32)[None], w_q.astype(bf16), k1.astype(bf16), k2.astype(bf16))


GT_LC = D_MODEL // 128


def _pack_kernel(u_ref, v_ref, o_ref):
    ub = pltpu.bitcast(u_ref[...].astype(bf16).astype(f32), jnp.uint32)
    vb = pltpu.bitcast(v_ref[...].astype(bf16).astype(f32), jnp.uint32)
    w = ub | (vb >> 16)
    for c in range(GT_LC):
        o_ref[:, c, :, :] = w[:, c * 128:(c + 1) * 128].reshape(w.shape[0], 1, 128)


def _pack_tables(u_emb, v_emb):
    n = u_emb.shape[0]
    tm = 256
    spec = pl.BlockSpec((tm, D_MODEL), lambda i: (i, 0))
    return pl.pallas_call(
        _pack_kernel,
        out_shape=jax.ShapeDtypeStruct((n, GT_LC, 1, 128), jnp.uint32),
        grid=(n // tm,),
        in_specs=[spec, spec],
        out_specs=pl.BlockSpec((tm, GT_LC, 1, 128), lambda i: (i, 0, 0, 0)),
        compiler_params=_cparams(("parallel",)),
        name="peer_pack",
    )(u_emb, v_emb)


GT_TQ = 8
GT_ROWS = GT_TQ * PEER_SLOTS


GT_SUB = 8
GT_RG = PEER_SLOTS // GT_SUB


def _row_copy(tbl_ref, buf_ref, sem, e, rg, s):
    return pltpu.make_async_copy(tbl_ref.at[e], buf_ref.at[rg, :, pl.ds(s, 1), :], sem)


def _expert_kernel(idx_ref, idxn_ref, gate_ref, hs_ref, gn_ref, gf_ref, tbl_ref, o_ref,
                   buf0_ref, buf1_ref, sem_ref, acc_ref):
    i = pl.program_id(0)
    nt = pl.num_programs(0)

    @pl.when(i == 0)
    def _():
        def body(j, carry):
            _row_copy(tbl_ref, buf0_ref, sem_ref.at[0], idx_ref[0, 0, j], j >> 3, j & 7).start()
            return carry
        lax.fori_loop(0, GT_ROWS, body, 0, unroll=8)

    def wait_all(buf_ref, sem):
        pltpu.make_async_copy(buf_ref, buf_ref, sem).wait()

    def step(cur_ref, cur_sem, nxt_ref, nxt_sem):
        wait_all(cur_ref, cur_sem)
        gate = gate_ref[0]
        h2 = _rms(hs_ref[...], gn_ref[...])
        hi_mask = jnp.uint32(0xFFFF0000)
        for t in range(GT_TQ):
            for k in range(PEER_SLOTS):
                j = t * PEER_SLOTS + k
                _row_copy(tbl_ref, nxt_ref, nxt_sem, idxn_ref[0, 0, j], j // GT_SUB, j % GT_SUB).start(priority=k % 2)
            hb = [jnp.broadcast_to(h2[t:t + 1, c * 128:(c + 1) * 128], (GT_SUB, 128)) for c in range(GT_LC)]
            parts = []
            for g in range(GT_RG):
                p = None
                for c in range(GT_LC):
                    u = pltpu.bitcast(cur_ref[t * GT_RG + g, c] & hi_mask, f32) * hb[c]
                    p = u if p is None else p + u
                parts.append(p)
            a = jnp.sum(jnp.concatenate(parts, axis=0), axis=1, keepdims=True)
            w = jnp.broadcast_to(gate[:, t:t + 1] * jax.nn.gelu(a), (PEER_SLOTS, 128))
            for c in range(GT_LC):
                o = None
                for g in range(GT_RG):
                    v = pltpu.bitcast(cur_ref[t * GT_RG + g, c] << 16, f32) * w[g * GT_SUB:(g + 1) * GT_SUB]
                    o = v if o is None else o + v
                acc_ref[t:t + 1, c * 128:(c + 1) * 128] = jnp.sum(o, axis=0, keepdims=True)
        o_ref[...] = _rms(hs_ref[...] + acc_ref[...], gf_ref[...])

        @pl.when(i == nt - 1)
        def _():
            wait_all(nxt_ref, nxt_sem)

    @pl.when(i % 2 == 0)
    def _():
        step(buf0_ref, sem_ref.at[0], buf1_ref, sem_ref.at[1])

    @pl.when(i % 2 == 1)
    def _():
        step(buf1_ref, sem_ref.at[1], buf0_ref, sem_ref.at[0])


def _experts(eidx, gate, hs2, norm_ffn, norm_g, tbl):
    rows = hs2.shape[0]
    nt = rows // GT_TQ
    idx3 = eidx.T.reshape(nt, 1, GT_ROWS)
    gate3 = jnp.transpose(gate.reshape(PEER_SLOTS, nt, GT_TQ), (1, 0, 2))
    row_spec = pl.BlockSpec((GT_TQ, D_MODEL), lambda i: (i, 0))
    buf = pltpu.VMEM((GT_ROWS // GT_SUB, GT_LC, GT_SUB, 128), jnp.uint32)
    return pl.pallas_call(
        _expert_kernel,
        out_shape=jax.ShapeDtypeStruct((rows, D_MODEL), f32),
        grid=(nt,),
        in_specs=[
            pl.BlockSpec((1, 1, GT_ROWS), lambda i: (i, 0, 0), memory_space=pltpu.SMEM),
            pl.BlockSpec((1, 1, GT_ROWS), lambda i: (jnp.minimum(i + 1, nt - 1), 0, 0), memory_space=pltpu.SMEM),
            pl.BlockSpec((1, PEER_SLOTS, GT_TQ), lambda i: (i, 0, 0)),
            row_spec,
            pl.BlockSpec((1, D_MODEL), lambda i: (0, 0)),
            pl.BlockSpec((1, D_MODEL), lambda i: (0, 0)),
            pl.BlockSpec(memory_space=pl.ANY),
        ],
        out_specs=row_spec,
        scratch_shapes=[
            buf,
            buf,
            pltpu.SemaphoreType.DMA((2,)),
            pltpu.VMEM((GT_TQ, D_MODEL), f32),
        ],
        compiler_params=_cparams(("arbitrary",), VMEM_LIMIT),
        name="peer_experts",
    )(idx3, idx3, gate3, hs2, norm_ffn.astype(f32)[None], norm_g.astype(f32)[None], tbl)


FS_NBUF = 4
FS_LOOK = FS_NBUF - 1


def _fused_kernel(ftok, hsc_ref, hsn_ref, gn_ref, gf_ref, wq_ref, k1_ref, k2_ref, tbl_ref, o_ref,
                  buf0_ref, buf1_ref, buf2_ref, buf3_ref, gsem_ref, eidx_ref, gate_ref, idx_ref, csem_ref, gsub_ref,
                  v1_ref, i1_ref, v2_ref, i2_ref, cand_ref, cidx_ref, sc_ref, acc_ref):
    nsub = ftok // GT_TQ
    bufs = (buf0_ref, buf1_ref, buf2_ref, buf3_ref)
    i = pl.program_id(0)
    nt = pl.num_programs(0)

    def route(hs, slot):
        _route_tile(hs, gn_ref, wq_ref, k1_ref, k2_ref, eidx_ref, gate_ref,
                    v1_ref, i1_ref, v2_ref, i2_ref, cand_ref, cidx_ref, sc_ref)
        cp = pltpu.make_async_copy(eidx_ref, idx_ref.at[slot], csem_ref)
        cp.start()
        gate = gate_ref[...]
        for sub in range(nsub):
            gsub_ref[slot, sub, :, 0:GT_TQ] = gate[:, sub * GT_TQ:(sub + 1) * GT_TQ]
        cp.wait()

    def issue_token(slot, sub, tt, b):
        for k in range(PEER_SLOTS):
            e = idx_ref[slot, k, sub * GT_TQ + tt]
            _row_copy(tbl_ref, bufs[b], gsem_ref.at[b], e, tt * GT_RG + k // GT_SUB, k % GT_SUB).start(priority=k % 2)

    def wait_buf(b):
        pltpu.make_async_copy(bufs[b], bufs[b], gsem_ref.at[b]).wait()

    @pl.when(i == 0)
    def _():
        route(hsc_ref[...], 0)
        for g in range(FS_LOOK):
            def body(tt, carry, g=g):
                issue_token(0, g, tt, g)
                return carry
            lax.fori_loop(0, GT_TQ, body, 0)

    @pl.when(i + 1 < nt)
    def _():
        route(hsn_ref[...], (i + 1) % 2)

    hi_mask = jnp.uint32(0xFFFF0000)

    def quad(jj, carry):
        for u in range(FS_NBUF):
            sub = jj * FS_NBUF + u
            ahead = sub + FS_LOOK
            over = (ahead >= nsub).astype(jnp.int32)
            aslot = (i + over) % 2
            asub = ahead - over * nsub
            ab = (u + FS_LOOK) % FS_NBUF
            wait_buf(u)
            cur_ref = bufs[u]
            row0 = pl.multiple_of(sub * GT_TQ, GT_TQ)
            hs = hsc_ref[pl.ds(row0, GT_TQ), :]
            h2 = _rms(hs, gn_ref[...])
            gate = gsub_ref[i % 2, sub]
            for t in range(GT_TQ):
                issue_token(aslot, asub, t, ab)
                hb = [jnp.broadcast_to(h2[t:t + 1, c * 128:(c + 1) * 128], (GT_SUB, 128)) for c in range(GT_LC)]
                parts = []
                for g in range(GT_RG):
                    p = None
                    for c in range(GT_LC):
                        uu = pltpu.bitcast(cur_ref[t * GT_RG + g, c] & hi_mask, f32) * hb[c]
                        p = uu if p is None else p + uu
                    parts.append(p)
                a = jnp.sum(jnp.concatenate(parts, axis=0), axis=1, keepdims=True)
                w = jnp.broadcast_to(gate[:, t:t + 1] * jax.nn.gelu(a), (PEER_SLOTS, 128))
                for c in range(GT_LC):
                    o = None
                    for g in range(GT_RG):
                        v = pltpu.bitcast(cur_ref[t * GT_RG + g, c] << 16, f32) * w[g * GT_SUB:(g + 1) * GT_SUB]
                        o = v if o is None else o + v
                    acc_ref[t:t + 1, c * 128:(c + 1) * 128] = jnp.sum(o, axis=0, keepdims=True)
            o_ref[pl.ds(row0, GT_TQ), :] = _rms(hs + acc_ref[...], gf_ref[...])
        return carry
    lax.fori_loop(0, nsub // FS_NBUF, quad, 0)

    @pl.when(i == nt - 1)
    def _():
        for b in range(FS_LOOK):
            wait_buf(b)


def _peer_fused(hs2, norm_ffn, norm_g, w_q, k1, k2, tbl, ftok=128):
    rows = hs2.shape[0]
    nt = rows // ftok
    nsub = ftok // GT_TQ
    assert nsub % FS_NBUF == 0 and nsub > FS_LOOK
    K = PEER_TOPK
    buf = pltpu.VMEM((GT_ROWS // GT_SUB, GT_LC, GT_SUB, 128), jnp.uint32)
    return pl.pallas_call(
        functools.partial(_fused_kernel, ftok),
        out_shape=jax.ShapeDtypeStruct((rows, D_MODEL), f32),
        grid=(nt,),
        in_specs=[
            pl.BlockSpec((ftok, D_MODEL), lambda i: (i, 0)),
            pl.BlockSpec((ftok, D_MODEL), lambda i: (jnp.minimum(i + 1, nt - 1), 0)),
            _resident((1, D_MODEL)),
            _resident((1, D_MODEL)),
            _resident((D_MODEL, PEER_HEADS * 2 * PEER_HALF)),
            _resident((PEER_HEADS, PEER_KEYS, PEER_HALF)),
            _resident((PEER_HEADS, PEER_KEYS, PEER_HALF)),
            pl.BlockSpec(memory_space=pl.ANY),
        ],
        out_specs=pl.BlockSpec((ftok, D_MODEL), lambda i: (i, 0)),
        scratch_shapes=[
            buf, buf, buf, buf,
            pltpu.SemaphoreType.DMA((FS_NBUF,)),
            pltpu.VMEM((PEER_SLOTS, ftok), jnp.int32),
            pltpu.VMEM((PEER_SLOTS, ftok), f32),
            pltpu.SMEM((2, PEER_SLOTS, ftok), jnp.int32),
            pltpu.SemaphoreType.DMA(()),
            pltpu.VMEM((2, nsub, PEER_SLOTS, 128), f32),
            pltpu.VMEM((K, ftok), f32), pltpu.VMEM((K, ftok), jnp.int32),
            pltpu.VMEM((K, ftok), f32), pltpu.VMEM((K, ftok), jnp.int32),
            pltpu.VMEM((_CAND_ROWS, ftok), f32), pltpu.VMEM((_CAND_ROWS, ftok), jnp.int32),
            pltpu.VMEM((K, ftok), f32),
            pltpu.VMEM((GT_TQ, D_MODEL), f32),
        ],
        compiler_params=_cparams(("arbitrary",), VMEM_LIMIT),
        name="peer_fused",
    )(hs2, hs2, norm_ffn.astype(f32)[None], norm_g.astype(f32)[None], w_q.astype(bf16), k1.astype(bf16), k2.astype(bf16), tbl)


def kernel(x, meta_tokens, norm_mix, w_in, b_gate, ssm_lam_re, ssm_lam_im, ssm_log_dt, ssm_b_re, ssm_b_im, ssm_c_re, ssm_c_im, ssm_d, ssm_w_glu, ssm_b_glu, attn_sink, w_proj_ssm, w_proj_attn, w_out, norm_ffn, peer_w_q, peer_k1, peer_k2, peer_u, peer_v, norm_final):
    nb, seq, d = x.shape
    assert w_in.shape[0] == 1 and d == D_MODEL and seq % GI_TM == 0
    l = 0
    lp = seq + PAD_FRONT
    nc = lp // Q

    front = jnp.concatenate([jnp.zeros((PAD_FRONT - N_META, d), x.dtype), meta_tokens.astype(x.dtype)], axis=0)
    ut, qs, kvs = _mixin(x, front, norm_mix[l], w_in[l])
    gates = _gatein(x.reshape(nb * seq, d), norm_mix[l], w_in[l], b_gate[l])

    kfull, wb, wc, coef = _s5_weights(ssm_lam_re[l], ssm_lam_im[l], ssm_log_dt[l], ssm_b_re[l], ssm_b_im[l],
                                      ssm_c_re[l], ssm_c_im[l])
    yt = _s5(ut, nb, nc, kfull, wb, wc, coef, ssm_d[l])

    yb = _attention(qs, kvs, attn_sink[l], nb, seq // BLOCK)

    hs2 = _merge(yt, yb, gates, x, ssm_w_glu[l], ssm_b_glu[l], w_proj_ssm[l], w_proj_attn[l], w_out[l])
    hs2 = hs2.reshape(nb * seq, d)

    tbl = _pack_tables(peer_u[l], peer_v[l])
    out = _peer_fused(hs2, norm_ffn[l], norm_final, peer_w_q[l], peer_k1[l], peer_k2[l], tbl)
    return out.reshape(nb, seq, d)
```

```python
import functools
import math

import numpy as np
import jax
import jax.numpy as jnp
from jax import lax
from jax.experimental import pallas as pl
from jax.experimental.pallas import tpu as pltpu

f32 = jnp.float32
bf16 = jnp.bfloat16

D_MODEL = 2048
N_META = 16
SSM_WIDTH = 1024
SSM_GROUP = 16
SSM_GROUPS = 64
SSM_STATE = 64
HEAD_DIM = 128
N_HEADS = 8
N_KV_HEADS = 2
ATTN_WIDTH = 1024
KV_WIDTH = 256
WINDOW = 128
BLOCK = 128
PEER_HEADS = 8
PEER_KEYS = 128
PEER_HALF = 128
PEER_TOPK = 16
PEER_SLOTS = PEER_HEADS * PEER_TOPK
NORM_EPS = 1e-6
NEG_INF = -1e30

PAD_FRONT = 256
VMEM_LIMIT = 56 * 1024 * 1024

HI = lax.Precision.HIGHEST


def _cparams(sem, vmem=None):
    return pltpu.CompilerParams(dimension_semantics=sem, vmem_limit_bytes=vmem)


def _rms(x, g):
    ms = jnp.mean(x * x, axis=-1, keepdims=True)
    return x * lax.rsqrt(ms + NORM_EPS) * g


def _dot_nt(a, b):
    return lax.dot_general(a, b, (((1,), (1,)), ((), ())), preferred_element_type=f32)


def _resident(shape):
    return pl.BlockSpec(shape, lambda *_: (0,) * len(shape), pipeline_mode=pl.Buffered(1))


MX_TM = PAD_FRONT


def _mixin_kernel(x_ref, front_ref, g_ref, wut_ref, wqkv_ref, ut_ref, q_ref, kv_ref):
    xin = jnp.where(pl.program_id(1) == 0, front_ref[...], x_ref[0])
    xn = _rms(xin, g_ref[...]).astype(bf16)
    ut_ref[...] = _dot_nt(wut_ref[...], xn).astype(bf16)
    z = jnp.dot(xn, wqkv_ref[...], preferred_element_type=f32)
    q_ref[...] = (z[:, :ATTN_WIDTH] * (HEAD_DIM ** -0.5)).astype(bf16)
    kv_ref[...] = z[:, ATTN_WIDTH:].astype(bf16)


def _mixin(x, front, norm_g, w_in):
    nb, seq, d = x.shape
    nt = seq // MX_TM + 1
    rows = nb * nt * MX_TM
    wut = w_in[:, :SSM_WIDTH].T.astype(bf16)
    wqkv = w_in[:, SSM_WIDTH:SSM_WIDTH + ATTN_WIDTH + 2 * KV_WIDTH].astype(bf16)
    return pl.pallas_call(
        _mixin_kernel,
        out_shape=(
            jax.ShapeDtypeStruct((SSM_WIDTH, rows), bf16),
            jax.ShapeDtypeStruct((rows, ATTN_WIDTH), bf16),
            jax.ShapeDtypeStruct((rows, 2 * KV_WIDTH), bf16),
        ),
        grid=(nb, nt),
        in_specs=[
            pl.BlockSpec((1, MX_TM, d), lambda b, i: (b, jnp.maximum(i - 1, 0), 0)),
            _resident((MX_TM, d)),
            _resident((1, d)),
            _resident((SSM_WIDTH, d)),
            _resident((d, ATTN_WIDTH + 2 * KV_WIDTH)),
        ],
        out_specs=(
            pl.BlockSpec((SSM_WIDTH, MX_TM), lambda b, i: (0, b * nt + i)),
            pl.BlockSpec((MX_TM, ATTN_WIDTH), lambda b, i: (b * nt + i, 0)),
            pl.BlockSpec((MX_TM, 2 * KV_WIDTH), lambda b, i: (b * nt + i, 0)),
        ),
        compiler_params=_cparams(("parallel", "parallel"), VMEM_LIMIT),
        name="mixer_in",
    )(x, front, norm_g.astype(f32)[None], wut, wqkv)


GI_TM = 512
GI_TN = 2048


def _gatein_kernel(x_ref, g_ref, w_ref, b_ref, o_ref, xn_ref):
    @pl.when(pl.program_id(1) == 0)
    def _():
        xn_ref[...] = _rms(x_ref[...], g_ref[...]).astype(bf16)

    z = jnp.dot(xn_ref[...], w_ref[...], preferred_element_type=f32)
    o_ref[...] = jax.nn.sigmoid(z + b_ref[...]).astype(bf16)


def _gatein(x2d, norm_g, w_in, b_gate):
    rows, d = x2d.shape
    wg = w_in[:, SSM_WIDTH + ATTN_WIDTH + 2 * KV_WIDTH:].astype(bf16)
    ng = wg.shape[1]
    return pl.pallas_call(
        _gatein_kernel,
        out_shape=jax.ShapeDtypeStruct((rows, ng), bf16),
        grid=(rows // GI_TM, ng // GI_TN),
        in_specs=[
            pl.BlockSpec((GI_TM, d), lambda i, j: (i, 0)),
            _resident((1, d)),
            pl.BlockSpec((d, GI_TN), lambda i, j: (0, j)),
            pl.BlockSpec((1, GI_TN), lambda i, j: (0, j)),
        ],
        out_specs=pl.BlockSpec((GI_TM, GI_TN), lambda i, j: (i, j)),
        scratch_shapes=[pltpu.VMEM((GI_TM, d), bf16)],
        compiler_params=_cparams(("parallel", "arbitrary"), VMEM_LIMIT),
        name="gate_in",
    )(x2d, norm_g.astype(f32)[None], wg, b_gate.astype(f32)[None])


Q = BLOCK
GW = SSM_GROUP * Q


def _s5_weights(lam_re, lam_im, log_dt, b_re, b_im, c_re, c_im):
    dt = jnp.exp(log_dt.astype(f32))[..., None]
    lr = lam_re.astype(f32)
    li = lam_im.astype(f32)
    k = jnp.arange(Q + 1, dtype=f32)
    mag = jnp.exp(lr[..., None] * dt[..., None] * k)
    ang = li[..., None] * dt[..., None] * k
    ar = mag * jnp.cos(ang)
    ai = mag * jnp.sin(ang)
    a1r, a1i = ar[..., 1], ai[..., 1]
    den = lr * lr + li * li
    nr, ni = a1r - 1.0, a1i
    cr = (nr * lr + ni * li) / den
    ci = (ni * lr - nr * li) / den
    br, bi = b_re.astype(f32), b_im.astype(f32)
    bbr = cr[..., None] * br - ci[..., None] * bi
    bbi = cr[..., None] * bi + ci[..., None] * br
    ccr = jnp.swapaxes(c_re.astype(f32), -1, -2)
    cci = jnp.swapaxes(c_im.astype(f32), -1, -2)

    cbr = ccr[..., :, None] * bbr[..., None, :] - cci[..., :, None] * bbi[..., None, :]
    cbi = ccr[..., :, None] * bbi[..., None, :] + cci[..., :, None] * bbr[..., None, :]
    G, P, C = cbr.shape[1], cbr.shape[2], cbr.shape[3]
    cbr2 = cbr.reshape(2, G, P, C * C)
    cbi2 = cbi.reshape(2, G, P, C * C)
    kk = (jnp.einsum('dgpk,dgpx->dgkx', ar[..., :Q], cbr2, precision=HI)
          - jnp.einsum('dgpk,dgpx->dgkx', ai[..., :Q], cbi2, precision=HI))
    kk = kk.reshape(2, G, Q, C, C)
    kf, kb = kk[0], kk[1]
    lag0 = (kf[:, 0] + kb[:, 0])[:, None]
    zero = jnp.zeros_like(lag0)
    kfull = jnp.concatenate([lag0, kf[:, 1:], zero, kb[:, 1:][:, ::-1]], axis=1)
    kfull = jnp.transpose(kfull, (0, 3, 2, 1))

    def bu_pow(d, pw_r, pw_i):
        re = pw_r[:, :, None, :] * bbr[d][..., None] - pw_i[:, :, None, :] * bbi[d][..., None]
        im = pw_r[:, :, None, :] * bbi[d][..., None] + pw_i[:, :, None, :] * bbr[d][..., None]
        return jnp.transpose(re, (0, 2, 3, 1)), jnp.transpose(im, (0, 2, 3, 1))
    fr, fi = bu_pow(0, ar[0][..., :Q][..., ::-1], ai[0][..., :Q][..., ::-1])
    rr, ri = bu_pow(1, ar[1][..., :Q], ai[1][..., :Q])
    wb = jnp.concatenate([fr, rr, fi, ri], axis=-1).reshape(G, C * Q, 4 * P)

    def c_pow(d, pw_r, pw_i):
        re = ccr[d][..., None] * pw_r[:, :, None, :] - cci[d][..., None] * pw_i[:, :, None, :]
        im = ccr[d][..., None] * pw_i[:, :, None, :] + cci[d][..., None] * pw_r[:, :, None, :]
        return re, -im
    f_re, f_im = c_pow(0, ar[0][..., 1:], ai[0][..., 1:])
    b_re_, b_im_ = c_pow(1, ar[1][..., 1:][..., ::-1], ai[1][..., 1:][..., ::-1])
    wc = jnp.concatenate([f_re, b_re_, f_im, b_im_], axis=1).reshape(G, 4 * P, C * Q)

    aqr, aqi = ar[..., Q], ai[..., Q]
    coef = jnp.stack([jnp.concatenate([aqr[0], aqr[1]], axis=-1),
                      jnp.concatenate([aqi[0], aqi[1]], axis=-1)], axis=1)
    coef = jnp.concatenate([coef, jnp.zeros((G, 6, 2 * P), f32)], axis=1)
    return kfull, wb.astype(bf16), wc.astype(bf16), coef


def _s5_kernel(nb, nc, ut_ref, kf_ref, wb_ref, wc_ref, coef_ref, dsk_ref, o_ref,
               x_ref, t_ref, cst_ref, hin_ref):
    C = SSM_GROUP
    P2 = 2 * SSM_STATE
    for c in range(C):
        x_ref[:, c * Q:(c + 1) * Q] = ut_ref[c]
    x = x_ref[...]

    def build(cp, carry):
        r0 = pl.multiple_of(cp * Q, Q)
        for c in range(C):
            row = kf_ref[0, cp, pl.ds(c, 1), :]
            blk = jnp.broadcast_to(row, (Q, 2 * Q))
            rolled = pltpu.roll(blk, 0, 1, stride=1, stride_axis=0)
            t_ref[pl.ds(r0, Q), c * Q:(c + 1) * Q] = rolled[:, :Q].astype(bf16)
        return carry
    lax.fori_loop(0, C, build, 0)

    cst_ref[...] = jnp.dot(x, wb_ref[0], preferred_element_type=f32)
    P = SSM_STATE
    aqr = coef_ref[0, 0:1, :]
    aqi = coef_ref[0, 1:2, :]
    is_fwd = lax.broadcasted_iota(jnp.int32, (1, P2), 1) < P
    for b in range(nb):
        hr = jnp.zeros((1, P2), f32)
        hi = jnp.zeros((1, P2), f32)
        for n in range(nc):
            rf = b * nc + n
            rb = b * nc + (nc - 1 - n)
            hin_ref[rf:rf + 1, 0:P] = hr[:, 0:P]
            hin_ref[rb:rb + 1, P:P2] = hr[:, P:P2]
            hin_ref[rf:rf + 1, P2:P2 + P] = hi[:, 0:P]
            hin_ref[rb:rb + 1, P2 + P:2 * P2] = hi[:, P:P2]
            cr = jnp.where(is_fwd, cst_ref[rf:rf + 1, 0:P2], cst_ref[rb:rb + 1, 0:P2])
            ci = jnp.where(is_fwd, cst_ref[rf:rf + 1, P2:2 * P2], cst_ref[rb:rb + 1, P2:2 * P2])
            hr, hi = aqr * hr - aqi * hi + cr, aqr * hi + aqi * hr + ci

    hin = hin_ref[...].astype(bf16)
    for cp in range(C // 2):
        cols = slice(cp * 2 * Q, (cp + 1) * 2 * Q)
        y2 = (jnp.dot(x, t_ref[:, cols], preferred_element_type=f32)
              + jnp.dot(hin, wc_ref[0, :, cols], preferred_element_type=f32))
        for h in range(2):
            c = 2 * cp + h
            y = y2[:, h * Q:(h + 1) * Q] + dsk_ref[0, c:c + 1, :] * ut_ref[c].astype(f32)
            o_ref[c] = jax.nn.gelu(y)


def _s5(ut, nb, nc, kfull, wb, wc, coef, d_skip):
    rows = nb * nc
    ut3 = ut.reshape(SSM_WIDTH, rows, Q)
    dsk = jnp.broadcast_to(d_skip.astype(f32).reshape(SSM_GROUPS, SSM_GROUP, 1), (SSM_GROUPS, SSM_GROUP, Q))
    C = SSM_GROUP
    out = pl.pallas_call(
        functools.partial(_s5_kernel, nb, nc),
        out_shape=jax.ShapeDtypeStruct((SSM_WIDTH, rows, Q), f32),
        grid=(SSM_GROUPS,),
        in_specs=[
            pl.BlockSpec((C, rows, Q), lambda g: (g, 0, 0)),
            pl.BlockSpec((1, C, C, 2 * Q), lambda g: (g, 0, 0, 0)),
            pl.BlockSpec((1, GW, 4 * SSM_STATE), lambda g: (g, 0, 0)),
            pl.BlockSpec((1, 4 * SSM_STATE, GW), lambda g: (g, 0, 0)),
            pl.BlockSpec((1, 8, 2 * SSM_STATE), lambda g: (g, 0, 0)),
            pl.BlockSpec((1, C, Q), lambda g: (g, 0, 0)),
        ],
        out_specs=pl.BlockSpec((C, rows, Q), lambda g: (g, 0, 0)),
        scratch_shapes=[
            pltpu.VMEM((rows, GW), bf16),
            pltpu.VMEM((GW, GW), bf16),
            pltpu.VMEM((rows, 4 * SSM_STATE), f32),
            pltpu.VMEM((rows, 4 * SSM_STATE), f32),
        ],
        compiler_params=_cparams(("parallel",), VMEM_LIMIT),
        name="s5_mixer",
    )(ut3, kfull, wb, wc, coef, dsk)
    return out.reshape(SSM_WIDTH, rows * Q)


def _alibi_slopes():
    return [float(2.0 ** (-8.0 * (h + 1) / N_HEADS)) for h in range(N_HEADS)]


def _attn_kernel(q_ref, km_ref, kp_ref, ko_ref, kn_ref, sink_ref, o_ref):
    jq = pl.program_id(1)
    nq = pl.num_programs(1)
    kv = jnp.concatenate([km_ref[...], kp_ref[...], ko_ref[...], kn_ref[...]], axis=0)
    nk = 4 * BLOCK
    row = lax.broadcasted_iota(jnp.int32, (BLOCK, nk), 0)
    col = lax.broadcasted_iota(jnp.int32, (BLOCK, nk), 1)
    seg = col // BLOCK
    cc = col % BLOCK
    rel = (seg - 2) * BLOCK + cc - row
    dist = jnp.abs(rel)
    band = (seg >= 1) & (dist <= WINDOW)
    band = band & jnp.logical_not((seg == 1) & (jq == 0))
    band = band & jnp.logical_not((seg == 3) & (jq == nq - 1))
    meta = (seg == 0) & (cc >= BLOCK - N_META)
    distf = dist.astype(f32)
    slopes = _alibi_slopes()
    grp = N_HEADS // N_KV_HEADS
    for h in range(N_HEADS):
        g = h // grp
        qh = q_ref[:, h * HEAD_DIM:(h + 1) * HEAD_DIM]
        kh = kv[:, g * HEAD_DIM:(g + 1) * HEAD_DIM]
        vh = kv[:, KV_WIDTH + g * HEAD_DIM:KV_WIDTH + (g + 1) * HEAD_DIM]
        s = _dot_nt(qh, kh)
        s = jnp.where(band, s - slopes[h] * distf, jnp.where(meta, s, NEG_INF))
        sink = sink_ref[h]
        m = jnp.maximum(jnp.max(s, axis=1, keepdims=True), sink)
        p = jnp.exp(s - m)
        den = jnp.sum(p, axis=1, keepdims=True) + jnp.exp(sink - m)
        o = jnp.dot(p.astype(bf16), vh, preferred_element_type=f32) / den
        o_ref[:, h * HEAD_DIM:(h + 1) * HEAD_DIM] = o.astype(bf16)


def _attention(qs, kvs, sink, nb, nq):
    pb = PAD_FRONT // BLOCK
    nbp = nq + pb
    kv_spec = lambda f: pl.BlockSpec((BLOCK, 2 * KV_WIDTH), f)
    return pl.pallas_call(
        _attn_kernel,
        out_shape=jax.ShapeDtypeStruct((nb * nq * BLOCK, ATTN_WIDTH), bf16),
        grid=(nb, nq),
        in_specs=[
            pl.BlockSpec((BLOCK, ATTN_WIDTH), lambda b, j: (b * nbp + pb + j, 0)),
            kv_spec(lambda b, j: (b * nbp + pb - 1, 0)),
            kv_spec(lambda b, j: (b * nbp + pb + j - 1, 0)),
            kv_spec(lambda b, j: (b * nbp + pb + j, 0)),
            kv_spec(lambda b, j: (b * nbp + jnp.minimum(pb + j + 1, nbp - 1), 0)),
            pl.BlockSpec(memory_space=pltpu.SMEM),
        ],
        out_specs=pl.BlockSpec((BLOCK, ATTN_WIDTH), lambda b, j: (b * nq + j, 0)),
        compiler_params=_cparams(("parallel", "parallel")),
        name="window_attn",
    )(qs, kvs, kvs, kvs, kvs, sink.astype(f32))


MG_TM = 256


def _merge_kernel(yt_ref, yb_ref, g0_ref, g1_ref, x_ref, wglu_ref, bglu_ref, wps_ref, wpa_ref, wout_ref, o_ref):
    y = yt_ref[...].T
    gl = jnp.dot(y.astype(bf16), wglu_ref[...], preferred_element_type=f32) + bglu_ref[...]
    ya = y * jax.nn.sigmoid(gl)
    pa = jnp.dot(ya.astype(bf16), wps_ref[...], preferred_element_type=f32)
    pbv = jnp.dot(yb_ref[...], wpa_ref[...], preferred_element_type=f32)
    merged = (g0_ref[...].astype(f32) * pa + g1_ref[...].astype(f32) * pbv).astype(bf16)
    o_ref[0] = x_ref[0] + jnp.dot(merged, wout_ref[...], preferred_element_type=f32)


def _merge(yt, yb, gates, x, w_glu, b_glu, w_ps, w_pa, w_out):
    nb, seq, d = x.shape
    tm = MG_TM
    nt = seq // tm
    lp_t = (seq + PAD_FRONT) // tm
    off = PAD_FRONT // tm
    return pl.pallas_call(
        _merge_kernel,
        out_shape=jax.ShapeDtypeStruct((nb, seq, d), f32),
        grid=(nb, nt),
        in_specs=[
            pl.BlockSpec((SSM_WIDTH, tm), lambda b, i: (0, b * lp_t + off + i)),
            pl.BlockSpec((tm, ATTN_WIDTH), lambda b, i: (b * nt + i, 0)),
            pl.BlockSpec((tm, d), lambda b, i: (b * nt + i, 0)),
            pl.BlockSpec((tm, d), lambda b, i: (b * nt + i, 1)),
            pl.BlockSpec((1, tm, d), lambda b, i: (b, i, 0)),
            _resident((SSM_WIDTH, SSM_WIDTH)),
            _resident((1, SSM_WIDTH)),
            _resident((SSM_WIDTH, d)),
            _resident((ATTN_WIDTH, d)),
            _resident((d, d)),
        ],
        out_specs=pl.BlockSpec((1, tm, d), lambda b, i: (b, i, 0)),
        compiler_params=_cparams(("parallel", "parallel"), VMEM_LIMIT),
        name="merge_out",
    )(yt, yb, gates, gates, x, w_glu.astype(bf16), b_glu.astype(f32)[None], w_ps.astype(bf16), w_pa.astype(bf16),
      w_out.astype(bf16))


RT_TM = 256


def _topk_rows(s, order, payload, vals_ref, pay_ref, r0):
    big = jnp.int32(2 ** 30)
    for i in range(PEER_TOPK):
        m = jnp.max(s, axis=0, keepdims=True)
        idx = jnp.min(jnp.where(s == m, order, big), axis=0, keepdims=True)
        sel = order == idx
        vals_ref[i:i + 1, :] = m
        if payload is None:
            pay_ref[r0 + i:r0 + i + 1, :] = idx
        else:
            pay_ref[r0 + i:r0 + i + 1, :] = jnp.max(jnp.where(sel, payload, -1), axis=0, keepdims=True)
        s = jnp.where(sel, -jnp.inf, s)


_CAND_BLOCKS = (
    ("a", 0, 0, 16, 0), ("a", 1, 0, 8, 0), ("a", 2, 0, 8, 0), ("a", 3, 0, 8, 0),
    ("b", 0, 0, 8, 4), ("b", 1, 0, 8, 4), ("b", 2, 0, 8, 4), ("b", 0, 8, 8, 0),
)
_CAND_ROWS = sum(blk[3] for blk in _CAND_BLOCKS)


def _route_kernel(hs_ref, g_ref, wq_ref, k1_ref, k2_ref, eidx_ref, gate_ref,
                  v1_ref, i1_ref, v2_ref, i2_ref, cand_ref, cidx_ref, sc_ref):
    _route_tile(hs_ref[...], g_ref, wq_ref, k1_ref, k2_ref, eidx_ref, gate_ref,
                v1_ref, i1_ref, v2_ref, i2_ref, cand_ref, cidx_ref, sc_ref)


def _route_tile(hs, g_ref, wq_ref, k1_ref, k2_ref, eidx_ref, gate_ref,
                v1_ref, i1_ref, v2_ref, i2_ref, cand_ref, cidx_ref, sc_ref):
    h2 = _rms(hs, g_ref[...])
    qf = jnp.dot(h2.astype(bf16), wq_ref[...], preferred_element_type=f32).astype(bf16)
    K = PEER_TOPK
    tm = qf.shape[0]
    keyid = lax.broadcasted_iota(jnp.int32, (PEER_KEYS, tm), 0)
    flat, dup = [], []
    for axis, fixed, first, n, drop in _CAND_BLOCKS:
        var = lax.broadcasted_iota(jnp.int32, (n, tm), 0) + first
        flat.append(fixed * K + var if axis == "a" else var * K + fixed)
        dup.append(var < drop)
    flat = jnp.concatenate(flat, axis=0)
    dup = jnp.concatenate(dup, axis=0)
    for h in range(PEER_HEADS):
        q1 = qf[:, h * 2 * PEER_HALF:h * 2 * PEER_HALF + PEER_HALF]
        q2 = qf[:, h * 2 * PEER_HALF + PEER_HALF:(h + 1) * 2 * PEER_HALF]
        s1 = _dot_nt(k1_ref[h], q1)
        s2 = _dot_nt(k2_ref[h], q2)
        _topk_rows(s1, keyid, None, v1_ref, i1_ref, 0)
        _topk_rows(s2, keyid, None, v2_ref, i2_ref, 0)
        r = 0
        for axis, fixed, first, n, drop in _CAND_BLOCKS:
            if axis == "a":
                va, ia = v1_ref[fixed:fixed + 1, :], i1_ref[fixed:fixed + 1, :]
                vb, ib = v2_ref[first:first + n, :], i2_ref[first:first + n, :]
            else:
                va, ia = v1_ref[first:first + n, :], i1_ref[first:first + n, :]
                vb, ib = v2_ref[fixed:fixed + 1, :], i2_ref[fixed:fixed + 1, :]
            cand_ref[r:r + n, :] = va + vb
            cidx_ref[r:r + n, :] = ia * PEER_KEYS + ib
            r += n
        cand = jnp.where(dup, -jnp.inf, cand_ref[...])
        _topk_rows(cand, flat, cidx_ref[...], sc_ref, eidx_ref, h * K)
        sc = sc_ref[...]
        e = jnp.exp(sc - sc[0:1, :])
        gate_ref[h * K:(h + 1) * K, :] = e / jnp.sum(e, axis=0, keepdims=True)


def _route(hs2, norm_g, w_q, k1, k2):
    rows = hs2.shape[0]
    tm = RT_TM
    K = PEER_TOPK
    return pl.pallas_call(
        _route_kernel,
        out_shape=(
            jax.ShapeDtypeStruct((PEER_SLOTS, rows), jnp.int32),
            jax.ShapeDtypeStruct((PEER_SLOTS, rows), f32),
        ),
        grid=(rows // tm,),
        in_specs=[
            pl.BlockSpec((tm, D_MODEL), lambda i: (i, 0)),
            _resident((1, D_MODEL)),
            _resident((D_MODEL, PEER_HEADS * 2 * PEER_HALF)),
            _resident((PEER_HEADS, PEER_KEYS, PEER_HALF)),
            _resident((PEER_HEADS, PEER_KEYS, PEER_HALF)),
        ],
        out_specs=(
            pl.BlockSpec((PEER_SLOTS, tm), lambda i: (0, i)),
            pl.BlockSpec((PEER_SLOTS, tm), lambda i: (0, i)),
        ),
        scratch_shapes=[
            pltpu.VMEM((K, tm), f32), pltpu.VMEM((K, tm), jnp.int32),
            pltpu.VMEM((K, tm), f32), pltpu.VMEM((K, tm), jnp.int32),
            pltpu.VMEM((_CAND_ROWS, tm), f32), pltpu.VMEM((_CAND_ROWS, tm), jnp.int32),
            pltpu.VMEM((K, tm), f32),
        ],
        compiler_params=_cparams(("parallel",), VMEM_LIMIT),
        name="peer_route",
    )(hs2, norm_g.astype(f32)[None], w_q.astype(bf16), k1.astype(bf16), k2.astype(bf16))


GT_LC = D_MODEL // 128


def _pack_kernel(u_ref, v_ref, o_ref):
    ub = pltpu.bitcast(u_ref[...].astype(bf16).astype(f32), jnp.uint32)
    vb = pltpu.bitcast(v_ref[...].astype(bf16).astype(f32), jnp.uint32)
    w = ub | (vb >> 16)
    for c in range(GT_LC):
        o_ref[:, c, :, :] = w[:, c * 128:(c + 1) * 128].reshape(w.shape[0], 1, 128)


def _pack_tables(u_emb, v_emb):
    n = u_emb.shape[0]
    tm = 256
    spec = pl.BlockSpec((tm, D_MODEL), lambda i: (i, 0))
    return pl.pallas_call(
        _pack_kernel,
        out_shape=jax.ShapeDtypeStruct((n, GT_LC, 1, 128), jnp.uint32),
        grid=(n // tm,),
        in_specs=[spec, spec],
        out_specs=pl.BlockSpec((tm, GT_LC, 1, 128), lambda i: (i, 0, 0, 0)),
        compiler_params=_cparams(("parallel",)),
        name="peer_pack",
    )(u_emb, v_emb)


GT_TQ = 8
GT_ROWS = GT_TQ * PEER_SLOTS


GT_SUB = 8
GT_RG = PEER_SLOTS // GT_SUB


def _row_copy(tbl_ref, buf_ref, sem, e, rg, s):
    return pltpu.make_async_copy(tbl_ref.at[e], buf_ref.at[rg, :, pl.ds(s, 1), :], sem)


def _expert_kernel(idx_ref, idxn_ref, gate_ref, hs_ref, gn_ref, gf_ref, tbl_ref, o_ref,
                   buf0_ref, buf1_ref, sem_ref, acc_ref):
    i = pl.program_id(0)
    nt = pl.num_programs(0)

    @pl.when(i == 0)
    def _():
        def body(j, carry):
            _row_copy(tbl_ref, buf0_ref, sem_ref.at[0], idx_ref[0, 0, j], j >> 3, j & 7).start()
            return carry
        lax.fori_loop(0, GT_ROWS, body, 0, unroll=8)

    def wait_all(buf_ref, sem):
        pltpu.make_async_copy(buf_ref, buf_ref, sem).wait()

    def step(cur_ref, cur_sem, nxt_ref, nxt_sem):
        wait_all(cur_ref, cur_sem)
        gate = gate_ref[0]
        h2 = _rms(hs_ref[...], gn_ref[...])
        hi_mask = jnp.uint32(0xFFFF0000)
        for t in range(GT_TQ):
            for k in range(PEER_SLOTS):
                j = t * PEER_SLOTS + k
                _row_copy(tbl_ref, nxt_ref, nxt_sem, idxn_ref[0, 0, j], j // GT_SUB, j % GT_SUB).start(priority=k % 2)
            hb = [jnp.broadcast_to(h2[t:t + 1, c * 128:(c + 1) * 128], (GT_SUB, 128)) for c in range(GT_LC)]
            parts = []
            for g in range(GT_RG):
                p = None
                for c in range(GT_LC):
                    u = pltpu.bitcast(cur_ref[t * GT_RG + g, c] & hi_mask, f32) * hb[c]
                    p = u if p is None else p + u
                parts.append(p)
            a = jnp.sum(jnp.concatenate(parts, axis=0), axis=1, keepdims=True)
            w = jnp.broadcast_to(gate[:, t:t + 1] * jax.nn.gelu(a), (PEER_SLOTS, 128))
            for c in range(GT_LC):
                o = None
                for g in range(GT_RG):
                    v = pltpu.bitcast(cur_ref[t * GT_RG + g, c] << 16, f32) * w[g * GT_SUB:(g + 1) * GT_SUB]
                    o = v if o is None else o + v
                acc_ref[t:t + 1, c * 128:(c + 1) * 128] = jnp.sum(o, axis=0, keepdims=True)
        o_ref[...] = _rms(hs_ref[...] + acc_ref[...], gf_ref[...])

        @pl.when(i == nt - 1)
        def _():
            wait_all(nxt_ref, nxt_sem)

    @pl.when(i % 2 == 0)
    def _():
        step(buf0_ref, sem_ref.at[0], buf1_ref, sem_ref.at[1])

    @pl.when(i % 2 == 1)
    def _():
        step(buf1_ref, sem_ref.at[1], buf0_ref, sem_ref.at[0])


def _experts(eidx, gate, hs2, norm_ffn, norm_g, tbl):
    rows = hs2.shape[0]
    nt = rows // GT_TQ
    idx3 = eidx.T.reshape(nt, 1, GT_ROWS)
    gate3 = jnp.transpose(gate.reshape(PEER_SLOTS, nt, GT_TQ), (1, 0, 2))
    row_spec = pl.BlockSpec((GT_TQ, D_MODEL), lambda i: (i, 0))
    buf = pltpu.VMEM((GT_ROWS // GT_SUB, GT_LC, GT_SUB, 128), jnp.uint32)
    return pl.pallas_call(
        _expert_kernel,
        out_shape=jax.ShapeDtypeStruct((rows, D_MODEL), f32),
        grid=(nt,),
        in_specs=[
            pl.BlockSpec((1, 1, GT_ROWS), lambda i: (i, 0, 0), memory_space=pltpu.SMEM),
            pl.BlockSpec((1, 1, GT_ROWS), lambda i: (jnp.minimum(i + 1, nt - 1), 0, 0), memory_space=pltpu.SMEM),
            pl.BlockSpec((1, PEER_SLOTS, GT_TQ), lambda i: (i, 0, 0)),
            row_spec,
            pl.BlockSpec((1, D_MODEL), lambda i: (0, 0)),
            pl.BlockSpec((1, D_MODEL), lambda i: (0, 0)),
            pl.BlockSpec(memory_space=pl.ANY),
        ],
        out_specs=row_spec,
        scratch_shapes=[
            buf,
            buf,
            pltpu.SemaphoreType.DMA((2,)),
            pltpu.VMEM((GT_TQ, D_MODEL), f32),
        ],
        compiler_params=_cparams(("arbitrary",), VMEM_LIMIT),
        name="peer_experts",
    )(idx3, idx3, gate3, hs2, norm_ffn.astype(f32)[None], norm_g.astype(f32)[None], tbl)


FS_NBUF = 4
FS_LOOK = FS_NBUF - 1


def _fused_kernel(ftok, hsc_ref, hsn_ref, gn_ref, gf_ref, wq_ref, k1_ref, k2_ref, tbl_ref, o_ref,
                  buf0_ref, buf1_ref, buf2_ref, buf3_ref, gsem_ref, eidx_ref, eidt_ref, gate_ref, idx_ref, csem_ref, gsub_ref,
                  v1_ref, i1_ref, v2_ref, i2_ref, cand_ref, cidx_ref, sc_ref, acc_ref):
    nsub = ftok // GT_TQ
    bufs = (buf0_ref, buf1_ref, buf2_ref, buf3_ref)
    i = pl.program_id(0)
    nt = pl.num_programs(0)

    def route(hs, slot):
        _route_tile(hs, gn_ref, wq_ref, k1_ref, k2_ref, eidx_ref, gate_ref,
                    v1_ref, i1_ref, v2_ref, i2_ref, cand_ref, cidx_ref, sc_ref)
        eidt_ref[...] = eidx_ref[...].T
        cp = pltpu.make_async_copy(eidt_ref, idx_ref.at[slot], csem_ref)
        cp.start()
        gate = gate_ref[...]
        for sub in range(nsub):
            gsub_ref[slot, sub, :, 0:GT_TQ] = gate[:, sub * GT_TQ:(sub + 1) * GT_TQ]
        cp.wait()

    def issue_token(slot, sub, tt, b):
        for k in range(PEER_SLOTS):
            e = idx_ref[slot, sub * GT_TQ + tt, k]
            _row_copy(tbl_ref, bufs[b], gsem_ref.at[b], e, tt * GT_RG + k // GT_SUB, k % GT_SUB).start(priority=k % 2)

    def wait_buf(b):
        pltpu.make_async_copy(bufs[b], bufs[b], gsem_ref.at[b]).wait()

    @pl.when(i == 0)
    def _():
        route(hsc_ref[...], 0)
        for g in range(FS_LOOK):
            def body(tt, carry, g=g):
                issue_token(0, g, tt, g)
                return carry
            lax.fori_loop(0, GT_TQ, body, 0)

    @pl.when(i + 1 < nt)
    def _():
        route(hsn_ref[...], (i + 1) % 2)

    hi_mask = jnp.uint32(0xFFFF0000)

    def quad(jj, carry):
        for u in range(FS_NBUF):
            sub = jj * FS_NBUF + u
            ahead = sub + FS_LOOK
            over = (ahead >= nsub).astype(jnp.int32)
            aslot = (i + over) % 2
            asub = ahead - over * nsub
            ab = (u + FS_LOOK) % FS_NBUF
            wait_buf(u)
            cur_ref = bufs[u]
            row0 = pl.multiple_of(sub * GT_TQ, GT_TQ)
            hs = hsc_ref[pl.ds(row0, GT_TQ), :]
            h2b = _rms(hs, gn_ref[...]).astype(bf16).astype(f32)
            gate = gsub_ref[i % 2, sub]
            for t in range(GT_TQ):
                issue_token(aslot, asub, t, ab)
                um = jnp.concatenate(
                    [jnp.concatenate([pltpu.bitcast(cur_ref[t * GT_RG + g, c] & hi_mask, f32) for c in range(GT_LC)], axis=1)
                     for g in range(GT_RG)], axis=0)
                a = _dot_nt(um, h2b)[:, t:t + 1]
                w = jnp.broadcast_to(gate[:, t:t + 1] * jax.nn.gelu(a), (PEER_SLOTS, 128))
                for c in range(GT_LC):
                    o = None
                    for g in range(GT_RG):
                        v = pltpu.bitcast(cur_ref[t * GT_RG + g, c] << 16, f32) * w[g * GT_SUB:(g + 1) * GT_SUB]
                        o = v if o is None else o + v
                    acc_ref[t:t + 1, c * 128:(c + 1) * 128] = jnp.sum(o, axis=0, keepdims=True)
            o_ref[pl.ds(row0, GT_TQ), :] = _rms(hs + acc_ref[...], gf_ref[...])
        return carry
    lax.fori_loop(0, nsub // FS_NBUF, quad, 0)

    @pl.when(i == nt - 1)
    def _():
        for b in range(FS_LOOK):
            wait_buf(b)


def _peer_fused(hs2, norm_ffn, norm_g, w_q, k1, k2, tbl, ftok=128):
    rows = hs2.shape[0]
    nt = rows // ftok
    nsub = ftok // GT_TQ
    assert nsub % FS_NBUF == 0 and nsub > FS_LOOK
    K = PEER_TOPK
    buf = pltpu.VMEM((GT_ROWS // GT_SUB, GT_LC, GT_SUB, 128), jnp.uint32)
    return pl.pallas_call(
        functools.partial(_fused_kernel, ftok),
        out_shape=jax.ShapeDtypeStruct((rows, D_MODEL), f32),
        grid=(nt,),
        in_specs=[
            pl.BlockSpec((ftok, D_MODEL), lambda i: (i, 0)),
            pl.BlockSpec((ftok, D_MODEL), lambda i: (jnp.minimum(i + 1, nt - 1), 0)),
            _resident((1, D_MODEL)),
            _resident((1, D_MODEL)),
            _resident((D_MODEL, PEER_HEADS * 2 * PEER_HALF)),
            _resident((PEER_HEADS, PEER_KEYS, PEER_HALF)),
            _resident((PEER_HEADS, PEER_KEYS, PEER_HALF)),
            pl.BlockSpec(memory_space=pl.ANY),
        ],
        out_specs=pl.BlockSpec((ftok, D_MODEL), lambda i: (i, 0)),
        scratch_shapes=[
            buf, buf, buf, buf,
            pltpu.SemaphoreType.DMA((FS_NBUF,)),
            pltpu.VMEM((PEER_SLOTS, ftok), jnp.int32),
            pltpu.VMEM((ftok, PEER_SLOTS), jnp.int32),
            pltpu.VMEM((PEER_SLOTS, ftok), f32),
            pltpu.SMEM((2, ftok, PEER_SLOTS), jnp.int32),
            pltpu.SemaphoreType.DMA(()),
            pltpu.VMEM((2, nsub, PEER_SLOTS, 128), f32),
            pltpu.VMEM((K, ftok), f32), pltpu.VMEM((K, ftok), jnp.int32),
            pltpu.VMEM((K, ftok), f32), pltpu.VMEM((K, ftok), jnp.int32),
            pltpu.VMEM((_CAND_ROWS, ftok), f32), pltpu.VMEM((_CAND_ROWS, ftok), jnp.int32),
            pltpu.VMEM((K, ftok), f32),
            pltpu.VMEM((GT_TQ, D_MODEL), f32),
        ],
        compiler_params=_cparams(("arbitrary",), VMEM_LIMIT),
        name="peer_fused",
    )(hs2, hs2, norm_ffn.astype(f32)[None], norm_g.astype(f32)[None], w_q.astype(bf16), k1.astype(bf16), k2.astype(bf16), tbl)


def kernel(x, meta_tokens, norm_mix, w_in, b_gate, ssm_lam_re, ssm_lam_im, ssm_log_dt, ssm_b_re, ssm_b_im, ssm_c_re, ssm_c_im, ssm_d, ssm_w_glu, ssm_b_glu, attn_sink, w_proj_ssm, w_proj_attn, w_out, norm_ffn, peer_w_q, peer_k1, peer_k2, peer_u, peer_v, norm_final):
    nb, seq, d = x.shape
    assert w_in.shape[0] == 1 and d == D_MODEL and seq % GI_TM == 0
    l = 0
    lp = seq + PAD_FRONT
    nc = lp // Q

    front = jnp.concatenate([jnp.zeros((PAD_FRONT - N_META, d), x.dtype), meta_tokens.astype(x.dtype)], axis=0)
    ut, qs, kvs = _mixin(x, front, norm_mix[l], w_in[l])
    gates = _gatein(x.reshape(nb * seq, d), norm_mix[l], w_in[l], b_gate[l])

    kfull, wb, wc, coef = _s5_weights(ssm_lam_re[l], ssm_lam_im[l], ssm_log_dt[l], ssm_b_re[l], ssm_b_im[l],
                                      ssm_c_re[l], ssm_c_im[l])
    yt = _s5(ut, nb, nc, kfull, wb, wc, coef, ssm_d[l])

    yb = _attention(qs, kvs, attn_sink[l], nb, seq // BLOCK)

    hs2 = _merge(yt, yb, gates, x, ssm_w_glu[l], ssm_b_glu[l], w_proj_ssm[l], w_proj_attn[l], w_out[l])
    hs2 = hs2.reshape(nb * seq, d)

    tbl = _pack_tables(peer_u[l], peer_v[l])
    out = _peer_fused(hs2, norm_ffn[l], norm_final, peer_w_q[l], peer_k1[l], peer_k2[l], tbl)
    return out.reshape(nb, seq, d)
```

```python
import functools
import math

import numpy as np
import jax
import jax.numpy as jnp
from jax import lax
from jax.experimental import pallas as pl
from jax.experimental.pallas import tpu as pltpu

f32 = jnp.float32
bf16 = jnp.bfloat16

D_MODEL = 2048
N_META = 16
SSM_WIDTH = 1024
SSM_GROUP = 16
SSM_GROUPS = 64
SSM_STATE = 64
HEAD_DIM = 128
N_HEADS = 8
N_KV_HEADS = 2
ATTN_WIDTH = 1024
KV_WIDTH = 256
WINDOW = 128
BLOCK = 128
PEER_HEADS = 8
PEER_KEYS = 128
PEER_HALF = 128
PEER_TOPK = 16
PEER_SLOTS = PEER_HEADS * PEER_TOPK
NORM_EPS = 1e-6
NEG_INF = -1e30

PAD_FRONT = 256
VMEM_LIMIT = 56 * 1024 * 1024

HI = lax.Precision.HIGHEST


def _cparams(sem, vmem=None):
    return pltpu.CompilerParams(dimension_semantics=sem, vmem_limit_bytes=vmem)


def _rms(x, g):
    ms = jnp.mean(x * x, axis=-1, keepdims=True)
    return x * lax.rsqrt(ms + NORM_EPS) * g


def _dot_nt(a, b):
    return lax.dot_general(a, b, (((1,), (1,)), ((), ())), preferred_element_type=f32)


def _resident(shape):
    return pl.BlockSpec(shape, lambda *_: (0,) * len(shape), pipeline_mode=pl.Buffered(1))


MX_TM = PAD_FRONT


def _mixin_kernel(x_ref, front_ref, g_ref, wut_ref, wqkv_ref, ut_ref, q_ref, kv_ref):
    xin = jnp.where(pl.program_id(1) == 0, front_ref[...], x_ref[0])
    xn = _rms(xin, g_ref[...]).astype(bf16)
    ut_ref[...] = _dot_nt(wut_ref[...], xn).astype(bf16)
    z = jnp.dot(xn, wqkv_ref[...], preferred_element_type=f32)
    q_ref[...] = (z[:, :ATTN_WIDTH] * (HEAD_DIM ** -0.5)).astype(bf16)
    kv_ref[...] = z[:, ATTN_WIDTH:].astype(bf16)


def _mixin(x, front, norm_g, w_in):
    nb, seq, d = x.shape
    nt = seq // MX_TM + 1
    rows = nb * nt * MX_TM
    wut = w_in[:, :SSM_WIDTH].T.astype(bf16)
    wqkv = w_in[:, SSM_WIDTH:SSM_WIDTH + ATTN_WIDTH + 2 * KV_WIDTH].astype(bf16)
    return pl.pallas_call(
        _mixin_kernel,
        out_shape=(
            jax.ShapeDtypeStruct((SSM_WIDTH, rows), bf16),
            jax.ShapeDtypeStruct((rows, ATTN_WIDTH), bf16),
            jax.ShapeDtypeStruct((rows, 2 * KV_WIDTH), bf16),
        ),
        grid=(nb, nt),
        in_specs=[
            pl.BlockSpec((1, MX_TM, d), lambda b, i: (b, jnp.maximum(i - 1, 0), 0)),
            _resident((MX_TM, d)),
            _resident((1, d)),
            _resident((SSM_WIDTH, d)),
            _resident((d, ATTN_WIDTH + 2 * KV_WIDTH)),
        ],
        out_specs=(
            pl.BlockSpec((SSM_WIDTH, MX_TM), lambda b, i: (0, b * nt + i)),
            pl.BlockSpec((MX_TM, ATTN_WIDTH), lambda b, i: (b * nt + i, 0)),
            pl.BlockSpec((MX_TM, 2 * KV_WIDTH), lambda b, i: (b * nt + i, 0)),
        ),
        compiler_params=_cparams(("parallel", "parallel"), VMEM_LIMIT),
        name="mixer_in",
    )(x, front, norm_g.astype(f32)[None], wut, wqkv)


GI_TM = 512
GI_TN = 2048


def _gatein_kernel(x_ref, g_ref, w_ref, b_ref, o_ref, xn_ref):
    @pl.when(pl.program_id(1) == 0)
    def _():
        xn_ref[...] = _rms(x_ref[...], g_ref[...]).astype(bf16)

    z = jnp.dot(xn_ref[...], w_ref[...], preferred_element_type=f32)
    o_ref[...] = jax.nn.sigmoid(z + b_ref[...]).astype(bf16)


def _gatein(x2d, norm_g, w_in, b_gate):
    rows, d = x2d.shape
    wg = w_in[:, SSM_WIDTH + ATTN_WIDTH + 2 * KV_WIDTH:].astype(bf16)
    ng = wg.shape[1]
    return pl.pallas_call(
        _gatein_kernel,
        out_shape=jax.ShapeDtypeStruct((rows, ng), bf16),
        grid=(rows // GI_TM, ng // GI_TN),
        in_specs=[
            pl.BlockSpec((GI_TM, d), lambda i, j: (i, 0)),
            _resident((1, d)),
            pl.BlockSpec((d, GI_TN), lambda i, j: (0, j)),
            pl.BlockSpec((1, GI_TN), lambda i, j: (0, j)),
        ],
        out_specs=pl.BlockSpec((GI_TM, GI_TN), lambda i, j: (i, j)),
        scratch_shapes=[pltpu.VMEM((GI_TM, d), bf16)],
        compiler_params=_cparams(("parallel", "arbitrary"), VMEM_LIMIT),
        name="gate_in",
    )(x2d, norm_g.astype(f32)[None], wg, b_gate.astype(f32)[None])


Q = BLOCK
GW = SSM_GROUP * Q


def _s5_weights(lam_re, lam_im, log_dt, b_re, b_im, c_re, c_im):
    dt = jnp.exp(log_dt.astype(f32))[..., None]
    lr = lam_re.astype(f32)
    li = lam_im.astype(f32)
    k = jnp.arange(Q + 1, dtype=f32)
    mag = jnp.exp(lr[..., None] * dt[..., None] * k)
    ang = li[..., None] * dt[..., None] * k
    ar = mag * jnp.cos(ang)
    ai = mag * jnp.sin(ang)
    a1r, a1i = ar[..., 1], ai[..., 1]
    den = lr * lr + li * li
    nr, ni = a1r - 1.0, a1i
    cr = (nr * lr + ni * li) / den
    ci = (ni * lr - nr * li) / den
    br, bi = b_re.astype(f32), b_im.astype(f32)
    bbr = cr[..., None] * br - ci[..., None] * bi
    bbi = cr[..., None] * bi + ci[..., None] * br
    ccr = jnp.swapaxes(c_re.astype(f32), -1, -2)
    cci = jnp.swapaxes(c_im.astype(f32), -1, -2)

    cbr = ccr[..., :, None] * bbr[..., None, :] - cci[..., :, None] * bbi[..., None, :]
    cbi = ccr[..., :, None] * bbi[..., None, :] + cci[..., :, None] * bbr[..., None, :]
    G, P, C = cbr.shape[1], cbr.shape[2], cbr.shape[3]
    cbr2 = cbr.reshape(2, G, P, C * C)
    cbi2 = cbi.reshape(2, G, P, C * C)
    kk = (jnp.einsum('dgpk,dgpx->dgkx', ar[..., :Q], cbr2, precision=HI)
          - jnp.einsum('dgpk,dgpx->dgkx', ai[..., :Q], cbi2, precision=HI))
    kk = kk.reshape(2, G, Q, C, C)
    kf, kb = kk[0], kk[1]
    lag0 = (kf[:, 0] + kb[:, 0])[:, None]
    zero = jnp.zeros_like(lag0)
    kfull = jnp.concatenate([lag0, kf[:, 1:], zero, kb[:, 1:][:, ::-1]], axis=1)
    kfull = jnp.transpose(kfull, (0, 3, 2, 1))

    def bu_pow(d, pw_r, pw_i):
        re = pw_r[:, :, None, :] * bbr[d][..., None] - pw_i[:, :, None, :] * bbi[d][..., None]
        im = pw_r[:, :, None, :] * bbi[d][..., None] + pw_i[:, :, None, :] * bbr[d][..., None]
        return jnp.transpose(re, (0, 2, 3, 1)), jnp.transpose(im, (0, 2, 3, 1))
    fr, fi = bu_pow(0, ar[0][..., :Q][..., ::-1], ai[0][..., :Q][..., ::-1])
    rr, ri = bu_pow(1, ar[1][..., :Q], ai[1][..., :Q])
    wb = jnp.concatenate([fr, rr, fi, ri], axis=-1).reshape(G, C * Q, 4 * P)

    def c_pow(d, pw_r, pw_i):
        re = ccr[d][..., None] * pw_r[:, :, None, :] - cci[d][..., None] * pw_i[:, :, None, :]
        im = ccr[d][..., None] * pw_i[:, :, None, :] + cci[d][..., None] * pw_r[:, :, None, :]
        return re, -im
    f_re, f_im = c_pow(0, ar[0][..., 1:], ai[0][..., 1:])
    b_re_, b_im_ = c_pow(1, ar[1][..., 1:][..., ::-1], ai[1][..., 1:][..., ::-1])
    wc = jnp.concatenate([f_re, b_re_, f_im, b_im_], axis=1).reshape(G, 4 * P, C * Q)

    aqr, aqi = ar[..., Q], ai[..., Q]
    coef = jnp.stack([jnp.concatenate([aqr[0], aqr[1]], axis=-1),
                      jnp.concatenate([aqi[0], aqi[1]], axis=-1)], axis=1)
    coef = jnp.concatenate([coef, jnp.zeros((G, 6, 2 * P), f32)], axis=1)
    return kfull, wb.astype(bf16), wc.astype(bf16), coef


def _s5_kernel(nb, nc, ut_ref, kf_ref, wb_ref, wc_ref, coef_ref, dsk_ref, o_ref,
               x_ref, t_ref, cst_ref, hin_ref):
    C = SSM_GROUP
    P2 = 2 * SSM_STATE
    for c in range(C):
        x_ref[:, c * Q:(c + 1) * Q] = ut_ref[c]
    x = x_ref[...]

    def build(cp, carry):
        r0 = pl.multiple_of(cp * Q, Q)
        for c in range(C):
            row = kf_ref[0, cp, pl.ds(c, 1), :]
            blk = jnp.broadcast_to(row, (Q, 2 * Q))
            rolled = pltpu.roll(blk, 0, 1, stride=1, stride_axis=0)
            t_ref[pl.ds(r0, Q), c * Q:(c + 1) * Q] = rolled[:, :Q].astype(bf16)
        return carry
    lax.fori_loop(0, C, build, 0)

    cst_ref[...] = jnp.dot(x, wb_ref[0], preferred_element_type=f32)
    P = SSM_STATE
    aqr = coef_ref[0, 0:1, :]
    aqi = coef_ref[0, 1:2, :]
    is_fwd = lax.broadcasted_iota(jnp.int32, (1, P2), 1) < P
    for b in range(nb):
        hr = jnp.zeros((1, P2), f32)
        hi = jnp.zeros((1, P2), f32)
        for n in range(nc):
            rf = b * nc + n
            rb = b * nc + (nc - 1 - n)
            hin_ref[rf:rf + 1, 0:P] = hr[:, 0:P]
            hin_ref[rb:rb + 1, P:P2] = hr[:, P:P2]
            hin_ref[rf:rf + 1, P2:P2 + P] = hi[:, 0:P]
            hin_ref[rb:rb + 1, P2 + P:2 * P2] = hi[:, P:P2]
            cr = jnp.where(is_fwd, cst_ref[rf:rf + 1, 0:P2], cst_ref[rb:rb + 1, 0:P2])
            ci = jnp.where(is_fwd, cst_ref[rf:rf + 1, P2:2 * P2], cst_ref[rb:rb + 1, P2:2 * P2])
            hr, hi = aqr * hr - aqi * hi + cr, aqr * hi + aqi * hr + ci

    hin = hin_ref[...].astype(bf16)
    for cp in range(C // 2):
        cols = slice(cp * 2 * Q, (cp + 1) * 2 * Q)
        y2 = (jnp.dot(x, t_ref[:, cols], preferred_element_type=f32)
              + jnp.dot(hin, wc_ref[0, :, cols], preferred_element_type=f32))
        for h in range(2):
            c = 2 * cp + h
            y = y2[:, h * Q:(h + 1) * Q] + dsk_ref[0, c:c + 1, :] * ut_ref[c].astype(f32)
            o_ref[c] = jax.nn.gelu(y)


def _s5(ut, nb, nc, kfull, wb, wc, coef, d_skip):
    rows = nb * nc
    ut3 = ut.reshape(SSM_WIDTH, rows, Q)
    dsk = jnp.broadcast_to(d_skip.astype(f32).reshape(SSM_GROUPS, SSM_GROUP, 1), (SSM_GROUPS, SSM_GROUP, Q))
    C = SSM_GROUP
    out = pl.pallas_call(
        functools.partial(_s5_kernel, nb, nc),
        out_shape=jax.ShapeDtypeStruct((SSM_WIDTH, rows, Q), f32),
        grid=(SSM_GROUPS,),
        in_specs=[
            pl.BlockSpec((C, rows, Q), lambda g: (g, 0, 0)),
            pl.BlockSpec((1, C, C, 2 * Q), lambda g: (g, 0, 0, 0)),
            pl.BlockSpec((1, GW, 4 * SSM_STATE), lambda g: (g, 0, 0)),
            pl.BlockSpec((1, 4 * SSM_STATE, GW), lambda g: (g, 0, 0)),
            pl.BlockSpec((1, 8, 2 * SSM_STATE), lambda g: (g, 0, 0)),
            pl.BlockSpec((1, C, Q), lambda g: (g, 0, 0)),
        ],
        out_specs=pl.BlockSpec((C, rows, Q), lambda g: (g, 0, 0)),
        scratch_shapes=[
            pltpu.VMEM((rows, GW), bf16),
            pltpu.VMEM((GW, GW), bf16),
            pltpu.VMEM((rows, 4 * SSM_STATE), f32),
            pltpu.VMEM((rows, 4 * SSM_STATE), f32),
        ],
        compiler_params=_cparams(("parallel",), VMEM_LIMIT),
        name="s5_mixer",
    )(ut3, kfull, wb, wc, coef, dsk)
    return out.reshape(SSM_WIDTH, rows * Q)


def _alibi_slopes():
    return [float(2.0 ** (-8.0 * (h + 1) / N_HEADS)) for h in range(N_HEADS)]


def _attn_kernel(q_ref, km_ref, kp_ref, ko_ref, kn_ref, sink_ref, o_ref):
    jq = pl.program_id(1)
    nq = pl.num_programs(1)
    kv = jnp.concatenate([km_ref[...], kp_ref[...], ko_ref[...], kn_ref[...]], axis=0)
    nk = 4 * BLOCK
    row = lax.broadcasted_iota(jnp.int32, (BLOCK, nk), 0)
    col = lax.broadcasted_iota(jnp.int32, (BLOCK, nk), 1)
    seg = col // BLOCK
    cc = col % BLOCK
    rel = (seg - 2) * BLOCK + cc - row
    dist = jnp.abs(rel)
    band = (seg >= 1) & (dist <= WINDOW)
    band = band & jnp.logical_not((seg == 1) & (jq == 0))
    band = band & jnp.logical_not((seg == 3) & (jq == nq - 1))
    meta = (seg == 0) & (cc >= BLOCK - N_META)
    distf = dist.astype(f32)
    slopes = _alibi_slopes()
    grp = N_HEADS // N_KV_HEADS
    for h in range(N_HEADS):
        g = h // grp
        qh = q_ref[:, h * HEAD_DIM:(h + 1) * HEAD_DIM]
        kh = kv[:, g * HEAD_DIM:(g + 1) * HEAD_DIM]
        vh = kv[:, KV_WIDTH + g * HEAD_DIM:KV_WIDTH + (g + 1) * HEAD_DIM]
        s = _dot_nt(qh, kh)
        s = jnp.where(band, s - slopes[h] * distf, jnp.where(meta, s, NEG_INF))
        sink = sink_ref[h]
        m = jnp.maximum(jnp.max(s, axis=1, keepdims=True), sink)
        p = jnp.exp(s - m)
        den = jnp.sum(p, axis=1, keepdims=True) + jnp.exp(sink - m)
        o = jnp.dot(p.astype(bf16), vh, preferred_element_type=f32) / den
        o_ref[:, h * HEAD_DIM:(h + 1) * HEAD_DIM] = o.astype(bf16)


def _attention(qs, kvs, sink, nb, nq):
    pb = PAD_FRONT // BLOCK
    nbp = nq + pb
    kv_spec = lambda f: pl.BlockSpec((BLOCK, 2 * KV_WIDTH), f)
    return pl.pallas_call(
        _attn_kernel,
        out_shape=jax.ShapeDtypeStruct((nb * nq * BLOCK, ATTN_WIDTH), bf16),
        grid=(nb, nq),
        in_specs=[
            pl.BlockSpec((BLOCK, ATTN_WIDTH), lambda b, j: (b * nbp + pb + j, 0)),
            kv_spec(lambda b, j: (b * nbp + pb - 1, 0)),
            kv_spec(lambda b, j: (b * nbp + pb + j - 1, 0)),
            kv_spec(lambda b, j: (b * nbp + pb + j, 0)),
            kv_spec(lambda b, j: (b * nbp + jnp.minimum(pb + j + 1, nbp - 1), 0)),
            pl.BlockSpec(memory_space=pltpu.SMEM),
        ],
        out_specs=pl.BlockSpec((BLOCK, ATTN_WIDTH), lambda b, j: (b * nq + j, 0)),
        compiler_params=_cparams(("parallel", "parallel")),
        name="window_attn",
    )(qs, kvs, kvs, kvs, kvs, sink.astype(f32))


MG_TM = 256


def _merge_kernel(yt_ref, yb_ref, g0_ref, g1_ref, x_ref, wglu_ref, bglu_ref, wps_ref, wpa_ref, wout_ref, o_ref):
    y = yt_ref[...].T
    gl = jnp.dot(y.astype(bf16), wglu_ref[...], preferred_element_type=f32) + bglu_ref[...]
    ya = y * jax.nn.sigmoid(gl)
    pa = jnp.dot(ya.astype(bf16), wps_ref[...], preferred_element_type=f32)
    pbv = jnp.dot(yb_ref[...], wpa_ref[...], preferred_element_type=f32)
    merged = (g0_ref[...].astype(f32) * pa + g1_ref[...].astype(f32) * pbv).astype(bf16)
    o_ref[0] = x_ref[0] + jnp.dot(merged, wout_ref[...], preferred_element_type=f32)


def _merge(yt, yb, gates, x, w_glu, b_glu, w_ps, w_pa, w_out):
    nb, seq, d = x.shape
    tm = MG_TM
    nt = seq // tm
    lp_t = (seq + PAD_FRONT) // tm
    off = PAD_FRONT // tm
    return pl.pallas_call(
        _merge_kernel,
        out_shape=jax.ShapeDtypeStruct((nb, seq, d), f32),
        grid=(nb, nt),
        in_specs=[
            pl.BlockSpec((SSM_WIDTH, tm), lambda b, i: (0, b * lp_t + off + i)),
            pl.BlockSpec((tm, ATTN_WIDTH), lambda b, i: (b * nt + i, 0)),
            pl.BlockSpec((tm, d), lambda b, i: (b * nt + i, 0)),
            pl.BlockSpec((tm, d), lambda b, i: (b * nt + i, 1)),
            pl.BlockSpec((1, tm, d), lambda b, i: (b, i, 0)),
            _resident((SSM_WIDTH, SSM_WIDTH)),
            _resident((1, SSM_WIDTH)),
            _resident((SSM_WIDTH, d)),
            _resident((ATTN_WIDTH, d)),
            _resident((d, d)),
        ],
        out_specs=pl.BlockSpec((1, tm, d), lambda b, i: (b, i, 0)),
        compiler_params=_cparams(("parallel", "parallel"), VMEM_LIMIT),
        name="merge_out",
    )(yt, yb, gates, gates, x, w_glu.astype(bf16), b_glu.astype(f32)[None], w_ps.astype(bf16), w_pa.astype(bf16),
      w_out.astype(bf16))


RT_TM = 256


def _topk_rows(s, order, payload, vals_ref, pay_ref, r0):
    big = jnp.int32(2 ** 30)
    for i in range(PEER_TOPK):
        m = jnp.max(s, axis=0, keepdims=True)
        idx = jnp.min(jnp.where(s == m, order, big), axis=0, keepdims=True)
        sel = order == idx
        vals_ref[i:i + 1, :] = m
        if payload is None:
            pay_ref[r0 + i:r0 + i + 1, :] = idx
        else:
            pay_ref[r0 + i:r0 + i + 1, :] = jnp.max(jnp.where(sel, payload, -1), axis=0, keepdims=True)
        s = jnp.where(sel, -jnp.inf, s)


_CAND_BLOCKS = (
    ("a", 0, 0, 16, 0), ("a", 1, 0, 8, 0), ("a", 2, 0, 8, 0), ("a", 3, 0, 8, 0),
    ("b", 0, 0, 8, 4), ("b", 1, 0, 8, 4), ("b", 2, 0, 8, 4), ("b", 0, 8, 8, 0),
)
_CAND_ROWS = sum(blk[3] for blk in _CAND_BLOCKS)


def _route_kernel(hs_ref, g_ref, wq_ref, k1_ref, k2_ref, eidx_ref, gate_ref,
                  v1_ref, i1_ref, v2_ref, i2_ref, cand_ref, cidx_ref, sc_ref):
    _route_tile(hs_ref[...], g_ref, wq_ref, k1_ref, k2_ref, eidx_ref, gate_ref,
                v1_ref, i1_ref, v2_ref, i2_ref, cand_ref, cidx_ref, sc_ref)


def _route_tile(hs, g_ref, wq_ref, k1_ref, k2_ref, eidx_ref, gate_ref,
                v1_ref, i1_ref, v2_ref, i2_ref, cand_ref, cidx_ref, sc_ref):
    h2 = _rms(hs, g_ref[...])
    qf = jnp.dot(h2.astype(bf16), wq_ref[...], preferred_element_type=f32).astype(bf16)
    K = PEER_TOPK
    tm = qf.shape[0]
    keyid = lax.broadcasted_iota(jnp.int32, (PEER_KEYS, tm), 0)
    flat, dup = [], []
    for axis, fixed, first, n, drop in _CAND_BLOCKS:
        var = lax.broadcasted_iota(jnp.int32, (n, tm), 0) + first
        flat.append(fixed * K + var if axis == "a" else var * K + fixed)
        dup.append(var < drop)
    flat = jnp.concatenate(flat, axis=0)
    dup = jnp.concatenate(dup, axis=0)
    for h in range(PEER_HEADS):
        q1 = qf[:, h * 2 * PEER_HALF:h * 2 * PEER_HALF + PEER_HALF]
        q2 = qf[:, h * 2 * PEER_HALF + PEER_HALF:(h + 1) * 2 * PEER_HALF]
        s1 = _dot_nt(k1_ref[h], q1)
        s2 = _dot_nt(k2_ref[h], q2)
        _topk_rows(s1, keyid, None, v1_ref, i1_ref, 0)
        _topk_rows(s2, keyid, None, v2_ref, i2_ref, 0)
        r = 0
        for axis, fixed, first, n, drop in _CAND_BLOCKS:
            if axis == "a":
                va, ia = v1_ref[fixed:fixed + 1, :], i1_ref[fixed:fixed + 1, :]
                vb, ib = v2_ref[first:first + n, :], i2_ref[first:first + n, :]
            else:
                va, ia = v1_ref[first:first + n, :], i1_ref[first:first + n, :]
                vb, ib = v2_ref[fixed:fixed + 1, :], i2_ref[fixed:fixed + 1, :]
            cand_ref[r:r + n, :] = va + vb
            cidx_ref[r:r + n, :] = ia * PEER_KEYS + ib
            r += n
        cand = jnp.where(dup, -jnp.inf, cand_ref[...])
        _topk_rows(cand, flat, cidx_ref[...], sc_ref, eidx_ref, h * K)
        sc = sc_ref[...]
        e = jnp.exp(sc - sc[0:1, :])
        gate_ref[h * K:(h + 1) * K, :] = e / jnp.sum(e, axis=0, keepdims=True)


def _route(hs2, norm_g, w_q, k1, k2):
    rows = hs2.shape[0]
    tm = RT_TM
    K = PEER_TOPK
    return pl.pallas_call(
        _route_kernel,
        out_shape=(
            jax.ShapeDtypeStruct((PEER_SLOTS, rows), jnp.int32),
            jax.ShapeDtypeStruct((PEER_SLOTS, rows), f32),
        ),
        grid=(rows // tm,),
        in_specs=[
            pl.BlockSpec((tm, D_MODEL), lambda i: (i, 0)),
            _resident((1, D_MODEL)),
            _resident((D_MODEL, PEER_HEADS * 2 * PEER_HALF)),
            _resident((PEER_HEADS, PEER_KEYS, PEER_HALF)),
            _resident((PEER_HEADS, PEER_KEYS, PEER_HALF)),
        ],
        out_specs=(
            pl.BlockSpec((PEER_SLOTS, tm), lambda i: (0, i)),
            pl.BlockSpec((PEER_SLOTS, tm), lambda i: (0, i)),
        ),
        scratch_shapes=[
            pltpu.VMEM((K, tm), f32), pltpu.VMEM((K, tm), jnp.int32),
            pltpu.VMEM((K, tm), f32), pltpu.VMEM((K, tm), jnp.int32),
            pltpu.VMEM((_CAND_ROWS, tm), f32), pltpu.VMEM((_CAND_ROWS, tm), jnp.int32),
            pltpu.VMEM((K, tm), f32),
        ],
        compiler_params=_cparams(("parallel",), VMEM_LIMIT),
        name="peer_route",
    )(hs2, norm_g.astype(f32)[None], w_q.astype(bf16), k1.astype(bf16), k2.astype(bf16))


GT_LC = D_MODEL // 128


def _pack_kernel(u_ref, v_ref, o_ref):
    ub = pltpu.bitcast(u_ref[...].astype(bf16).astype(f32), jnp.uint32)
    vb = pltpu.bitcast(v_ref[...].astype(bf16).astype(f32), jnp.uint32)
    w = ub | (vb >> 16)
    tm = w.shape[0]
    for c in range(GT_LC):
        o_ref[pl.ds(c, tm, stride=GT_LC), :] = w[:, c * 128:(c + 1) * 128]


def _pack_tables(u_emb, v_emb):
    n = u_emb.shape[0]
    tm = 256
    spec = pl.BlockSpec((tm, D_MODEL), lambda i: (i, 0))
    return pl.pallas_call(
        _pack_kernel,
        out_shape=jax.ShapeDtypeStruct((n * GT_LC, 128), jnp.uint32),
        grid=(n // tm,),
        in_specs=[spec, spec],
        out_specs=pl.BlockSpec((tm * GT_LC, 128), lambda i: (i, 0)),
        compiler_params=_cparams(("parallel",)),
        name="peer_pack",
    )(u_emb, v_emb)


GT_TQ = 8
GT_ROWS = GT_TQ * PEER_SLOTS


GT_SUB = 8
GT_RG = PEER_SLOTS // GT_SUB


def _row_copy(tbl_ref, buf_ref, sem, e, rg, s):
    return pltpu.make_async_copy(tbl_ref.at[e], buf_ref.at[rg, :, pl.ds(s, 1), :], sem)


def _expert_kernel(idx_ref, idxn_ref, gate_ref, hs_ref, gn_ref, gf_ref, tbl_ref, o_ref,
                   buf0_ref, buf1_ref, sem_ref, acc_ref):
    i = pl.program_id(0)
    nt = pl.num_programs(0)

    @pl.when(i == 0)
    def _():
        def body(j, carry):
            _row_copy(tbl_ref, buf0_ref, sem_ref.at[0], idx_ref[0, 0, j], j >> 3, j & 7).start()
            return carry
        lax.fori_loop(0, GT_ROWS, body, 0, unroll=8)

    def wait_all(buf_ref, sem):
        pltpu.make_async_copy(buf_ref, buf_ref, sem).wait()

    def step(cur_ref, cur_sem, nxt_ref, nxt_sem):
        wait_all(cur_ref, cur_sem)
        gate = gate_ref[0]
        h2 = _rms(hs_ref[...], gn_ref[...])
        hi_mask = jnp.uint32(0xFFFF0000)
        for t in range(GT_TQ):
            for k in range(PEER_SLOTS):
                j = t * PEER_SLOTS + k
                _row_copy(tbl_ref, nxt_ref, nxt_sem, idxn_ref[0, 0, j], j // GT_SUB, j % GT_SUB).start(priority=k % 2)
            hb = [jnp.broadcast_to(h2[t:t + 1, c * 128:(c + 1) * 128], (GT_SUB, 128)) for c in range(GT_LC)]
            parts = []
            for g in range(GT_RG):
                p = None
                for c in range(GT_LC):
                    u = pltpu.bitcast(cur_ref[t * GT_RG + g, c] & hi_mask, f32) * hb[c]
                    p = u if p is None else p + u
                parts.append(p)
            a = jnp.sum(jnp.concatenate(parts, axis=0), axis=1, keepdims=True)
            w = jnp.broadcast_to(gate[:, t:t + 1] * jax.nn.gelu(a), (PEER_SLOTS, 128))
            for c in range(GT_LC):
                o = None
                for g in range(GT_RG):
                    v = pltpu.bitcast(cur_ref[t * GT_RG + g, c] << 16, f32) * w[g * GT_SUB:(g + 1) * GT_SUB]
                    o = v if o is None else o + v
                acc_ref[t:t + 1, c * 128:(c + 1) * 128] = jnp.sum(o, axis=0, keepdims=True)
        o_ref[...] = _rms(hs_ref[...] + acc_ref[...], gf_ref[...])

        @pl.when(i == nt - 1)
        def _():
            wait_all(nxt_ref, nxt_sem)

    @pl.when(i % 2 == 0)
    def _():
        step(buf0_ref, sem_ref.at[0], buf1_ref, sem_ref.at[1])

    @pl.when(i % 2 == 1)
    def _():
        step(buf1_ref, sem_ref.at[1], buf0_ref, sem_ref.at[0])


def _experts(eidx, gate, hs2, norm_ffn, norm_g, tbl):
    rows = hs2.shape[0]
    nt = rows // GT_TQ
    idx3 = eidx.T.reshape(nt, 1, GT_ROWS)
    gate3 = jnp.transpose(gate.reshape(PEER_SLOTS, nt, GT_TQ), (1, 0, 2))
    row_spec = pl.BlockSpec((GT_TQ, D_MODEL), lambda i: (i, 0))
    buf = pltpu.VMEM((GT_ROWS // GT_SUB, GT_LC, GT_SUB, 128), jnp.uint32)
    return pl.pallas_call(
        _expert_kernel,
        out_shape=jax.ShapeDtypeStruct((rows, D_MODEL), f32),
        grid=(nt,),
        in_specs=[
            pl.BlockSpec((1, 1, GT_ROWS), lambda i: (i, 0, 0), memory_space=pltpu.SMEM),
            pl.BlockSpec((1, 1, GT_ROWS), lambda i: (jnp.minimum(i + 1, nt - 1), 0, 0), memory_space=pltpu.SMEM),
            pl.BlockSpec((1, PEER_SLOTS, GT_TQ), lambda i: (i, 0, 0)),
            row_spec,
            pl.BlockSpec((1, D_MODEL), lambda i: (0, 0)),
            pl.BlockSpec((1, D_MODEL), lambda i: (0, 0)),
            pl.BlockSpec(memory_space=pl.ANY),
        ],
        out_specs=row_spec,
        scratch_shapes=[
            buf,
            buf,
            pltpu.SemaphoreType.DMA((2,)),
            pltpu.VMEM((GT_TQ, D_MODEL), f32),
        ],
        compiler_params=_cparams(("arbitrary",), VMEM_LIMIT),
        name="peer_experts",
    )(idx3, idx3, gate3, hs2, norm_ffn.astype(f32)[None], norm_g.astype(f32)[None], tbl)


FS_NBUF = 4
FS_LOOK = FS_NBUF - 1
FS_PITCH = GT_LC + 1
FS_VMEM_LIMIT = 62 * 1024 * 1024


def _fused_kernel(ftok, hsc_ref, hsn_ref, gn_ref, gf_ref, wq_ref, k1_ref, k2_ref, tbl_ref, o_ref,
                  buf0_ref, buf1_ref, buf2_ref, buf3_ref, gsem_ref, eidx_ref, eidt_ref, gate_ref, idx_ref, csem_ref, gsub_ref,
                  v1_ref, i1_ref, v2_ref, i2_ref, cand_ref, cidx_ref, sc_ref, acc_ref):
    nsub = ftok // GT_TQ
    bufs = (buf0_ref, buf1_ref, buf2_ref, buf3_ref)
    i = pl.program_id(0)
    nt = pl.num_programs(0)

    def route(hs, slot):
        _route_tile(hs, gn_ref, wq_ref, k1_ref, k2_ref, eidx_ref, gate_ref,
                    v1_ref, i1_ref, v2_ref, i2_ref, cand_ref, cidx_ref, sc_ref)
        eidt_ref[...] = eidx_ref[...].T * GT_LC
        cp = pltpu.make_async_copy(eidt_ref, idx_ref.at[slot], csem_ref)
        cp.start()
        gate = gate_ref[...]
        for sub in range(nsub):
            gsub_ref[slot, sub, :, 0:GT_TQ] = gate[:, sub * GT_TQ:(sub + 1) * GT_TQ]
        cp.wait()

    def issue_token(slot, sub, tt, b):
        for k in range(PEER_SLOTS):
            line = pl.multiple_of(idx_ref[slot, sub * GT_TQ + tt, k], GT_LC)
            pltpu.make_async_copy(tbl_ref.at[pl.ds(line, GT_LC), :],
                                  bufs[b].at[pl.ds((tt * PEER_SLOTS + k) * FS_PITCH, GT_LC), :],
                                  gsem_ref.at[b]).start(priority=k % 2)

    def wait_buf(b):
        span = bufs[b].at[pl.ds(0, GT_ROWS * GT_LC), :]
        pltpu.make_async_copy(span, span, gsem_ref.at[b]).wait()

    @pl.when(i == 0)
    def _():
        route(hsc_ref[...], 0)
        for g in range(FS_LOOK):
            for tt in range(GT_TQ):
                issue_token(0, g, tt, g)

    @pl.when(i + 1 < nt)
    def _():
        route(hsn_ref[...], (i + 1) % 2)

    hi_mask = jnp.uint32(0xFFFF0000)

    def tile(buf_ref, t, g, c):
        return buf_ref[pl.ds((t * GT_RG + g) * GT_SUB * FS_PITCH + c, GT_SUB, stride=FS_PITCH), :]

    def quad(jj, carry):
        for u in range(FS_NBUF):
            sub = jj * FS_NBUF + u
            ahead = sub + FS_LOOK
            over = jnp.where(ahead >= nsub, 1, 0)
            aslot = (i + over) % 2
            asub = ahead - over * nsub
            ab = (u + FS_LOOK) % FS_NBUF
            wait_buf(u)
            cur_ref = bufs[u]
            row0 = pl.multiple_of(sub * GT_TQ, GT_TQ)
            hs = hsc_ref[pl.ds(row0, GT_TQ), :]
            h2b = _rms(hs, gn_ref[...]).astype(bf16).astype(f32)
            gate = gsub_ref[i % 2, sub]
            for t in range(GT_TQ):
                issue_token(aslot, asub, t, ab)
                um = jnp.concatenate(
                    [jnp.concatenate([pltpu.bitcast(tile(cur_ref, t, g, c) & hi_mask, f32) for c in range(GT_LC)], axis=1)
                     for g in range(GT_RG)], axis=0)
                a = _dot_nt(um, h2b)[:, t:t + 1]
                w = jnp.broadcast_to(gate[:, t:t + 1] * jax.nn.gelu(a), (PEER_SLOTS, 128))
                for c in range(GT_LC):
                    o = None
                    for g in range(GT_RG):
                        v = pltpu.bitcast(tile(cur_ref, t, g, c) << 16, f32) * w[g * GT_SUB:(g + 1) * GT_SUB]
                        o = v if o is None else o + v
                    acc_ref[t:t + 1, c * 128:(c + 1) * 128] = jnp.sum(o, axis=0, keepdims=True)
            o_ref[pl.ds(row0, GT_TQ), :] = _rms(hs + acc_ref[...], gf_ref[...])
        return carry
    lax.fori_loop(0, nsub // FS_NBUF, quad, 0)

    @pl.when(i == nt - 1)
    def _():
        for b in range(FS_LOOK):
            wait_buf(b)


def _peer_fused(hs2, norm_ffn, norm_g, w_q, k1, k2, tbl, ftok=128):
    rows = hs2.shape[0]
    nt = rows // ftok
    nsub = ftok // GT_TQ
    assert nsub % FS_NBUF == 0 and nsub > FS_LOOK
    K = PEER_TOPK
    buf = pltpu.VMEM((GT_ROWS * FS_PITCH, 128), jnp.uint32)
    return pl.pallas_call(
        functools.partial(_fused_kernel, ftok),
        out_shape=jax.ShapeDtypeStruct((rows, D_MODEL), f32),
        grid=(nt,),
        in_specs=[
            pl.BlockSpec((ftok, D_MODEL), lambda i: (i, 0)),
            pl.BlockSpec((ftok, D_MODEL), lambda i: (jnp.minimum(i + 1, nt - 1), 0)),
            _resident((1, D_MODEL)),
            _resident((1, D_MODEL)),
            _resident((D_MODEL, PEER_HEADS * 2 * PEER_HALF)),
            _resident((PEER_HEADS, PEER_KEYS, PEER_HALF)),
            _resident((PEER_HEADS, PEER_KEYS, PEER_HALF)),
            pl.BlockSpec(memory_space=pl.ANY),
        ],
        out_specs=pl.BlockSpec((ftok, D_MODEL), lambda i: (i, 0)),
        scratch_shapes=[
            buf, buf, buf, buf,
            pltpu.SemaphoreType.DMA((FS_NBUF,)),
            pltpu.VMEM((PEER_SLOTS, ftok), jnp.int32),
            pltpu.VMEM((ftok, PEER_SLOTS), jnp.int32),
            pltpu.VMEM((PEER_SLOTS, ftok), f32),
            pltpu.SMEM((2, ftok, PEER_SLOTS), jnp.int32),
            pltpu.SemaphoreType.DMA(()),
            pltpu.VMEM((2, nsub, PEER_SLOTS, 128), f32),
            pltpu.VMEM((K, ftok), f32), pltpu.VMEM((K, ftok), jnp.int32),
            pltpu.VMEM((K, ftok), f32), pltpu.VMEM((K, ftok), jnp.int32),
            pltpu.VMEM((_CAND_ROWS, ftok), f32), pltpu.VMEM((_CAND_ROWS, ftok), jnp.int32),
            pltpu.VMEM((K, ftok), f32),
            pltpu.VMEM((GT_TQ, D_MODEL), f32),
        ],
        compiler_params=_cparams(("arbitrary",), FS_VMEM_LIMIT),
        name="peer_fused",
    )(hs2, hs2, norm_ffn.astype(f32)[None], norm_g.astype(f32)[None], w_q.astype(bf16), k1.astype(bf16), k2.astype(bf16), tbl)


def kernel(x, meta_tokens, norm_mix, w_in, b_gate, ssm_lam_re, ssm_lam_im, ssm_log_dt, ssm_b_re, ssm_b_im, ssm_c_re, ssm_c_im, ssm_d, ssm_w_glu, ssm_b_glu, attn_sink, w_proj_ssm, w_proj_attn, w_out, norm_ffn, peer_w_q, peer_k1, peer_k2, peer_u, peer_v, norm_final):
    nb, seq, d = x.shape
    assert w_in.shape[0] == 1 and d == D_MODEL and seq % GI_TM == 0
    l = 0
    lp = seq + PAD_FRONT
    nc = lp // Q

    front = jnp.concatenate([jnp.zeros((PAD_FRONT - N_META, d), x.dtype), meta_tokens.astype(x.dtype)], axis=0)
    ut, qs, kvs = _mixin(x, front, norm_mix[l], w_in[l])
    gates = _gatein(x.reshape(nb * seq, d), norm_mix[l], w_in[l], b_gate[l])

    kfull, wb, wc, coef = _s5_weights(ssm_lam_re[l], ssm_lam_im[l], ssm_log_dt[l], ssm_b_re[l], ssm_b_im[l],
                                      ssm_c_re[l], ssm_c_im[l])
    yt = _s5(ut, nb, nc, kfull, wb, wc, coef, ssm_d[l])

    yb = _attention(qs, kvs, attn_sink[l], nb, seq // BLOCK)

    hs2 = _merge(yt, yb, gates, x, ssm_w_glu[l], ssm_b_glu[l], w_proj_ssm[l], w_proj_attn[l], w_out[l])
    hs2 = hs2.reshape(nb * seq, d)

    tbl = _pack_tables(peer_u[l], peer_v[l])
    out = _peer_fused(hs2, norm_ffn[l], norm_final, peer_w_q[l], peer_k1[l], peer_k2[l], tbl)
    return out.reshape(nb, seq, d)
```

```python
import functools
import math

import numpy as np
import jax
import jax.numpy as jnp
from jax import lax
from jax.experimental import pallas as pl
from jax.experimental.pallas import tpu as pltpu

f32 = jnp.float32
bf16 = jnp.bfloat16

D_MODEL = 2048
N_META = 16
SSM_WIDTH = 1024
SSM_GROUP = 16
SSM_GROUPS = 64
SSM_STATE = 64
HEAD_DIM = 128
N_HEADS = 8
N_KV_HEADS = 2
ATTN_WIDTH = 1024
KV_WIDTH = 256
WINDOW = 128
BLOCK = 128
PEER_HEADS = 8
PEER_KEYS = 128
PEER_HALF = 128
PEER_TOPK = 16
PEER_SLOTS = PEER_HEADS * PEER_TOPK
NORM_EPS = 1e-6
NEG_INF = -1e30

PAD_FRONT = 256
VMEM_LIMIT = 56 * 1024 * 1024

HI = lax.Precision.HIGHEST


def _cparams(sem, vmem=None):
    return pltpu.CompilerParams(dimension_semantics=sem, vmem_limit_bytes=vmem)


def _rms(x, g):
    ms = jnp.mean(x * x, axis=-1, keepdims=True)
    return x * lax.rsqrt(ms + NORM_EPS) * g


def _dot_nt(a, b):
    return lax.dot_general(a, b, (((1,), (1,)), ((), ())), preferred_element_type=f32)


def _resident(shape):
    return pl.BlockSpec(shape, lambda *_: (0,) * len(shape), pipeline_mode=pl.Buffered(1))


MX_TM = PAD_FRONT


def _mixin_kernel(x_ref, front_ref, g_ref, wut_ref, wqkv_ref, ut_ref, q_ref, kv_ref):
    xin = jnp.where(pl.program_id(1) == 0, front_ref[...], x_ref[0])
    xn = _rms(xin, g_ref[...]).astype(bf16)
    ut_ref[...] = _dot_nt(wut_ref[...], xn).astype(bf16)
    z = jnp.dot(xn, wqkv_ref[...], preferred_element_type=f32)
    q_ref[...] = (z[:, :ATTN_WIDTH] * (HEAD_DIM ** -0.5)).astype(bf16)
    kv_ref[...] = z[:, ATTN_WIDTH:].astype(bf16)


def _mixin(x, front, norm_g, w_in):
    nb, seq, d = x.shape
    nt = seq // MX_TM + 1
    rows = nb * nt * MX_TM
    wut = w_in[:, :SSM_WIDTH].T.astype(bf16)
    wqkv = w_in[:, SSM_WIDTH:SSM_WIDTH + ATTN_WIDTH + 2 * KV_WIDTH].astype(bf16)
    return pl.pallas_call(
        _mixin_kernel,
        out_shape=(
            jax.ShapeDtypeStruct((SSM_WIDTH, rows), bf16),
            jax.ShapeDtypeStruct((rows, ATTN_WIDTH), bf16),
            jax.ShapeDtypeStruct((rows, 2 * KV_WIDTH), bf16),
        ),
        grid=(nb, nt),
        in_specs=[
            pl.BlockSpec((1, MX_TM, d), lambda b, i: (b, jnp.maximum(i - 1, 0), 0)),
            _resident((MX_TM, d)),
            _resident((1, d)),
            _resident((SSM_WIDTH, d)),
            _resident((d, ATTN_WIDTH + 2 * KV_WIDTH)),
        ],
        out_specs=(
            pl.BlockSpec((SSM_WIDTH, MX_TM), lambda b, i: (0, b * nt + i)),
            pl.BlockSpec((MX_TM, ATTN_WIDTH), lambda b, i: (b * nt + i, 0)),
            pl.BlockSpec((MX_TM, 2 * KV_WIDTH), lambda b, i: (b * nt + i, 0)),
        ),
        compiler_params=_cparams(("parallel", "parallel"), VMEM_LIMIT),
        name="mixer_in",
    )(x, front, norm_g.astype(f32)[None], wut, wqkv)


GI_TM = 512
GI_TN = 2048


def _gatein_kernel(x_ref, g_ref, w_ref, b_ref, o_ref, xn_ref):
    @pl.when(pl.program_id(1) == 0)
    def _():
        xn_ref[...] = _rms(x_ref[...], g_ref[...]).astype(bf16)

    z = jnp.dot(xn_ref[...], w_ref[...], preferred_element_type=f32)
    o_ref[...] = jax.nn.sigmoid(z + b_ref[...]).astype(bf16)


def _gatein(x2d, norm_g, w_in, b_gate):
    rows, d = x2d.shape
    wg = w_in[:, SSM_WIDTH + ATTN_WIDTH + 2 * KV_WIDTH:].astype(bf16)
    ng = wg.shape[1]
    return pl.pallas_call(
        _gatein_kernel,
        out_shape=jax.ShapeDtypeStruct((rows, ng), bf16),
        grid=(rows // GI_TM, ng // GI_TN),
        in_specs=[
            pl.BlockSpec((GI_TM, d), lambda i, j: (i, 0)),
            _resident((1, d)),
            pl.BlockSpec((d, GI_TN), lambda i, j: (0, j)),
            pl.BlockSpec((1, GI_TN), lambda i, j: (0, j)),
        ],
        out_specs=pl.BlockSpec((GI_TM, GI_TN), lambda i, j: (i, j)),
        scratch_shapes=[pltpu.VMEM((GI_TM, d), bf16)],
        compiler_params=_cparams(("parallel", "arbitrary"), VMEM_LIMIT),
        name="gate_in",
    )(x2d, norm_g.astype(f32)[None], wg, b_gate.astype(f32)[None])


Q = BLOCK
GW = SSM_GROUP * Q


def _s5_weights(lam_re, lam_im, log_dt, b_re, b_im, c_re, c_im):
    dt = jnp.exp(log_dt.astype(f32))[..., None]
    lr = lam_re.astype(f32)
    li = lam_im.astype(f32)
    k = jnp.arange(Q + 1, dtype=f32)
    mag = jnp.exp(lr[..., None] * dt[..., None] * k)
    ang = li[..., None] * dt[..., None] * k
    ar = mag * jnp.cos(ang)
    ai = mag * jnp.sin(ang)
    a1r, a1i = ar[..., 1], ai[..., 1]
    den = lr * lr + li * li
    nr, ni = a1r - 1.0, a1i
    cr = (nr * lr + ni * li) / den
    ci = (ni * lr - nr * li) / den
    br, bi = b_re.astype(f32), b_im.astype(f32)
    bbr = cr[..., None] * br - ci[..., None] * bi
    bbi = cr[..., None] * bi + ci[..., None] * br
    ccr = jnp.swapaxes(c_re.astype(f32), -1, -2)
    cci = jnp.swapaxes(c_im.astype(f32), -1, -2)

    cbr = ccr[..., :, None] * bbr[..., None, :] - cci[..., :, None] * bbi[..., None, :]
    cbi = ccr[..., :, None] * bbi[..., None, :] + cci[..., :, None] * bbr[..., None, :]
    G, P, C = cbr.shape[1], cbr.shape[2], cbr.shape[3]
    cbr2 = cbr.reshape(2, G, P, C * C)
    cbi2 = cbi.reshape(2, G, P, C * C)
    kk = (jnp.einsum('dgpk,dgpx->dgkx', ar[..., :Q], cbr2, precision=HI)
          - jnp.einsum('dgpk,dgpx->dgkx', ai[..., :Q], cbi2, precision=HI))
    kk = kk.reshape(2, G, Q, C, C)
    kf, kb = kk[0], kk[1]
    lag0 = (kf[:, 0] + kb[:, 0])[:, None]
    zero = jnp.zeros_like(lag0)
    kfull = jnp.concatenate([lag0, kf[:, 1:], zero, kb[:, 1:][:, ::-1]], axis=1)
    kfull = jnp.transpose(kfull, (0, 3, 2, 1))

    def bu_pow(d, pw_r, pw_i):
        re = pw_r[:, :, None, :] * bbr[d][..., None] - pw_i[:, :, None, :] * bbi[d][..., None]
        im = pw_r[:, :, None, :] * bbi[d][..., None] + pw_i[:, :, None, :] * bbr[d][..., None]
        return jnp.transpose(re, (0, 2, 3, 1)), jnp.transpose(im, (0, 2, 3, 1))
    fr, fi = bu_pow(0, ar[0][..., :Q][..., ::-1], ai[0][..., :Q][..., ::-1])
    rr, ri = bu_pow(1, ar[1][..., :Q], ai[1][..., :Q])
    wb = jnp.concatenate([fr, rr, fi, ri], axis=-1).reshape(G, C * Q, 4 * P)

    def c_pow(d, pw_r, pw_i):
        re = ccr[d][..., None] * pw_r[:, :, None, :] - cci[d][..., None] * pw_i[:, :, None, :]
        im = ccr[d][..., None] * pw_i[:, :, None, :] + cci[d][..., None] * pw_r[:, :, None, :]
        return re, -im
    f_re, f_im = c_pow(0, ar[0][..., 1:], ai[0][..., 1:])
    b_re_, b_im_ = c_pow(1, ar[1][..., 1:][..., ::-1], ai[1][..., 1:][..., ::-1])
    wc = jnp.concatenate([f_re, b_re_, f_im, b_im_], axis=1).reshape(G, 4 * P, C * Q)

    aqr, aqi = ar[..., Q], ai[..., Q]
    coef = jnp.stack([jnp.concatenate([aqr[0], aqr[1]], axis=-1),
                      jnp.concatenate([aqi[0], aqi[1]], axis=-1)], axis=1)
    coef = jnp.concatenate([coef, jnp.zeros((G, 6, 2 * P), f32)], axis=1)
    return kfull, wb.astype(bf16), wc.astype(bf16), coef


def _s5_kernel(nb, nc, ut_ref, kf_ref, wb_ref, wc_ref, coef_ref, dsk_ref, o_ref,
               x_ref, t_ref, cst_ref, hin_ref):
    C = SSM_GROUP
    P2 = 2 * SSM_STATE
    for c in range(C):
        x_ref[:, c * Q:(c + 1) * Q] = ut_ref[c]
    x = x_ref[...]

    def build(cp, carry):
        r0 = pl.multiple_of(cp * Q, Q)
        for c in range(C):
            row = kf_ref[0, cp, pl.ds(c, 1), :]
            blk = jnp.broadcast_to(row, (Q, 2 * Q))
            rolled = pltpu.roll(blk, 0, 1, stride=1, stride_axis=0)
            t_ref[pl.ds(r0, Q), c * Q:(c + 1) * Q] = rolled[:, :Q].astype(bf16)
        return carry
    lax.fori_loop(0, C, build, 0)

    cst_ref[...] = jnp.dot(x, wb_ref[0], preferred_element_type=f32)
    P = SSM_STATE
    aqr = coef_ref[0, 0:1, :]
    aqi = coef_ref[0, 1:2, :]
    is_fwd = lax.broadcasted_iota(jnp.int32, (1, P2), 1) < P
    for b in range(nb):
        hr = jnp.zeros((1, P2), f32)
        hi = jnp.zeros((1, P2), f32)
        for n in range(nc):
            rf = b * nc + n
            rb = b * nc + (nc - 1 - n)
            hin_ref[rf:rf + 1, 0:P] = hr[:, 0:P]
            hin_ref[rb:rb + 1, P:P2] = hr[:, P:P2]
            hin_ref[rf:rf + 1, P2:P2 + P] = hi[:, 0:P]
            hin_ref[rb:rb + 1, P2 + P:2 * P2] = hi[:, P:P2]
            cr = jnp.where(is_fwd, cst_ref[rf:rf + 1, 0:P2], cst_ref[rb:rb + 1, 0:P2])
            ci = jnp.where(is_fwd, cst_ref[rf:rf + 1, P2:2 * P2], cst_ref[rb:rb + 1, P2:2 * P2])
            hr, hi = aqr * hr - aqi * hi + cr, aqr * hi + aqi * hr + ci

    hin = hin_ref[...].astype(bf16)
    for cp in range(C // 2):
        cols = slice(cp * 2 * Q, (cp + 1) * 2 * Q)
        y2 = (jnp.dot(x, t_ref[:, cols], preferred_element_type=f32)
              + jnp.dot(hin, wc_ref[0, :, cols], preferred_element_type=f32))
        for h in range(2):
            c = 2 * cp + h
            y = y2[:, h * Q:(h + 1) * Q] + dsk_ref[0, c:c + 1, :] * ut_ref[c].astype(f32)
            o_ref[c] = jax.nn.gelu(y)


def _s5(ut, nb, nc, kfull, wb, wc, coef, d_skip):
    rows = nb * nc
    ut3 = ut.reshape(SSM_WIDTH, rows, Q)
    dsk = jnp.broadcast_to(d_skip.astype(f32).reshape(SSM_GROUPS, SSM_GROUP, 1), (SSM_GROUPS, SSM_GROUP, Q))
    C = SSM_GROUP
    out = pl.pallas_call(
        functools.partial(_s5_kernel, nb, nc),
        out_shape=jax.ShapeDtypeStruct((SSM_WIDTH, rows, Q), f32),
        grid=(SSM_GROUPS,),
        in_specs=[
            pl.BlockSpec((C, rows, Q), lambda g: (g, 0, 0)),
            pl.BlockSpec((1, C, C, 2 * Q), lambda g: (g, 0, 0, 0)),
            pl.BlockSpec((1, GW, 4 * SSM_STATE), lambda g: (g, 0, 0)),
            pl.BlockSpec((1, 4 * SSM_STATE, GW), lambda g: (g, 0, 0)),
            pl.BlockSpec((1, 8, 2 * SSM_STATE), lambda g: (g, 0, 0)),
            pl.BlockSpec((1, C, Q), lambda g: (g, 0, 0)),
        ],
        out_specs=pl.BlockSpec((C, rows, Q), lambda g: (g, 0, 0)),
        scratch_shapes=[
            pltpu.VMEM((rows, GW), bf16),
            pltpu.VMEM((GW, GW), bf16),
            pltpu.VMEM((rows, 4 * SSM_STATE), f32),
            pltpu.VMEM((rows, 4 * SSM_STATE), f32),
        ],
        compiler_params=_cparams(("parallel",), VMEM_LIMIT),
        name="s5_mixer",
    )(ut3, kfull, wb, wc, coef, dsk)
    return out.reshape(SSM_WIDTH, rows * Q)


def _alibi_slopes():
    return [float(2.0 ** (-8.0 * (h + 1) / N_HEADS)) for h in range(N_HEADS)]


def _attn_kernel(q_ref, km_ref, kp_ref, ko_ref, kn_ref, sink_ref, o_ref):
    jq = pl.program_id(1)
    nq = pl.num_programs(1)
    kv = jnp.concatenate([km_ref[...], kp_ref[...], ko_ref[...], kn_ref[...]], axis=0)
    nk = 4 * BLOCK
    row = lax.broadcasted_iota(jnp.int32, (BLOCK, nk), 0)
    col = lax.broadcasted_iota(jnp.int32, (BLOCK, nk), 1)
    seg = col // BLOCK
    cc = col % BLOCK
    rel = (seg - 2) * BLOCK + cc - row
    dist = jnp.abs(rel)
    band = (seg >= 1) & (dist <= WINDOW)
    band = band & jnp.logical_not((seg == 1) & (jq == 0))
    band = band & jnp.logical_not((seg == 3) & (jq == nq - 1))
    meta = (seg == 0) & (cc >= BLOCK - N_META)
    distf = dist.astype(f32)
    slopes = _alibi_slopes()
    grp = N_HEADS // N_KV_HEADS
    for h in range(N_HEADS):
        g = h // grp
        qh = q_ref[:, h * HEAD_DIM:(h + 1) * HEAD_DIM]
        kh = kv[:, g * HEAD_DIM:(g + 1) * HEAD_DIM]
        vh = kv[:, KV_WIDTH + g * HEAD_DIM:KV_WIDTH + (g + 1) * HEAD_DIM]
        s = _dot_nt(qh, kh)
        s = jnp.where(band, s - slopes[h] * distf, jnp.where(meta, s, NEG_INF))
        sink = sink_ref[h]
        m = jnp.maximum(jnp.max(s, axis=1, keepdims=True), sink)
        p = jnp.exp(s - m)
        den = jnp.sum(p, axis=1, keepdims=True) + jnp.exp(sink - m)
        o = jnp.dot(p.astype(bf16), vh, preferred_element_type=f32) / den
        o_ref[:, h * HEAD_DIM:(h + 1) * HEAD_DIM] = o.astype(bf16)


def _attention(qs, kvs, sink, nb, nq):
    pb = PAD_FRONT // BLOCK
    nbp = nq + pb
    kv_spec = lambda f: pl.BlockSpec((BLOCK, 2 * KV_WIDTH), f)
    return pl.pallas_call(
        _attn_kernel,
        out_shape=jax.ShapeDtypeStruct((nb * nq * BLOCK, ATTN_WIDTH), bf16),
        grid=(nb, nq),
        in_specs=[
            pl.BlockSpec((BLOCK, ATTN_WIDTH), lambda b, j: (b * nbp + pb + j, 0)),
            kv_spec(lambda b, j: (b * nbp + pb - 1, 0)),
            kv_spec(lambda b, j: (b * nbp + pb + j - 1, 0)),
            kv_spec(lambda b, j: (b * nbp + pb + j, 0)),
            kv_spec(lambda b, j: (b * nbp + jnp.minimum(pb + j + 1, nbp - 1), 0)),
            pl.BlockSpec(memory_space=pltpu.SMEM),
        ],
        out_specs=pl.BlockSpec((BLOCK, ATTN_WIDTH), lambda b, j: (b * nq + j, 0)),
        compiler_params=_cparams(("parallel", "parallel")),
        name="window_attn",
    )(qs, kvs, kvs, kvs, kvs, sink.astype(f32))


MG_TM = 256


def _merge_kernel(yt_ref, yb_ref, g0_ref, g1_ref, x_ref, wglu_ref, bglu_ref, wps_ref, wpa_ref, wout_ref, o_ref):
    y = yt_ref[...].T
    gl = jnp.dot(y.astype(bf16), wglu_ref[...], preferred_element_type=f32) + bglu_ref[...]
    ya = y * jax.nn.sigmoid(gl)
    pa = jnp.dot(ya.astype(bf16), wps_ref[...], preferred_element_type=f32)
    pbv = jnp.dot(yb_ref[...], wpa_ref[...], preferred_element_type=f32)
    merged = (g0_ref[...].astype(f32) * pa + g1_ref[...].astype(f32) * pbv).astype(bf16)
    o_ref[0] = x_ref[0] + jnp.dot(merged, wout_ref[...], preferred_element_type=f32)


def _merge(yt, yb, gates, x, w_glu, b_glu, w_ps, w_pa, w_out):
    nb, seq, d = x.shape
    tm = MG_TM
    nt = seq // tm
    lp_t = (seq + PAD_FRONT) // tm
    off = PAD_FRONT // tm
    return pl.pallas_call(
        _merge_kernel,
        out_shape=jax.ShapeDtypeStruct((nb, seq, d), f32),
        grid=(nb, nt),
        in_specs=[
            pl.BlockSpec((SSM_WIDTH, tm), lambda b, i: (0, b * lp_t + off + i)),
            pl.BlockSpec((tm, ATTN_WIDTH), lambda b, i: (b * nt + i, 0)),
            pl.BlockSpec((tm, d), lambda b, i: (b * nt + i, 0)),
            pl.BlockSpec((tm, d), lambda b, i: (b * nt + i, 1)),
            pl.BlockSpec((1, tm, d), lambda b, i: (b, i, 0)),
            _resident((SSM_WIDTH, SSM_WIDTH)),
            _resident((1, SSM_WIDTH)),
            _resident((SSM_WIDTH, d)),
            _resident((ATTN_WIDTH, d)),
            _resident((d, d)),
        ],
        out_specs=pl.BlockSpec((1, tm, d), lambda b, i: (b, i, 0)),
        compiler_params=_cparams(("parallel", "parallel"), VMEM_LIMIT),
        name="merge_out",
    )(yt, yb, gates, gates, x, w_glu.astype(bf16), b_glu.astype(f32)[None], w_ps.astype(bf16), w_pa.astype(bf16),
      w_out.astype(bf16))


def _topk_rows(s, order, payload, vals_ref, pay_ref, r0):
    big = jnp.int32(2 ** 30)
    for i in range(PEER_TOPK):
        m = jnp.max(s, axis=0, keepdims=True)
        idx = jnp.min(jnp.where(s == m, order, big), axis=0, keepdims=True)
        sel = order == idx
        vals_ref[i:i + 1, :] = m
        if payload is None:
            pay_ref[r0 + i:r0 + i + 1, :] = idx
        else:
            pay_ref[r0 + i:r0 + i + 1, :] = jnp.max(jnp.where(sel, payload, -1), axis=0, keepdims=True)
        s = jnp.where(sel, -jnp.inf, s)


_CAND_BLOCKS = (
    ("a", 0, 0, 16, 0), ("a", 1, 0, 8, 0), ("a", 2, 0, 8, 0), ("a", 3, 0, 8, 0),
    ("b", 0, 0, 8, 4), ("b", 1, 0, 8, 4), ("b", 2, 0, 8, 4), ("b", 0, 8, 8, 0),
)
_CAND_ROWS = sum(blk[3] for blk in _CAND_BLOCKS)


def _route_tile(hs, g_ref, wq_ref, k1_ref, k2_ref, eidx_ref, gate_ref,
                v1_ref, i1_ref, v2_ref, i2_ref, cand_ref, cidx_ref, sc_ref):
    h2 = _rms(hs, g_ref[...])
    qf = jnp.dot(h2.astype(bf16), wq_ref[...], preferred_element_type=f32).astype(bf16)
    K = PEER_TOPK
    tm = qf.shape[0]
    keyid = lax.broadcasted_iota(jnp.int32, (PEER_KEYS, tm), 0)
    flat, dup = [], []
    for axis, fixed, first, n, drop in _CAND_BLOCKS:
        var = lax.broadcasted_iota(jnp.int32, (n, tm), 0) + first
        flat.append(fixed * K + var if axis == "a" else var * K + fixed)
        dup.append(var < drop)
    flat = jnp.concatenate(flat, axis=0)
    dup = jnp.concatenate(dup, axis=0)
    for h in range(PEER_HEADS):
        q1 = qf[:, h * 2 * PEER_HALF:h * 2 * PEER_HALF + PEER_HALF]
        q2 = qf[:, h * 2 * PEER_HALF + PEER_HALF:(h + 1) * 2 * PEER_HALF]
        s1 = _dot_nt(k1_ref[h], q1)
        s2 = _dot_nt(k2_ref[h], q2)
        _topk_rows(s1, keyid, None, v1_ref, i1_ref, 0)
        _topk_rows(s2, keyid, None, v2_ref, i2_ref, 0)
        r = 0
        for axis, fixed, first, n, drop in _CAND_BLOCKS:
            if axis == "a":
                va, ia = v1_ref[fixed:fixed + 1, :], i1_ref[fixed:fixed + 1, :]
                vb, ib = v2_ref[first:first + n, :], i2_ref[first:first + n, :]
            else:
                va, ia = v1_ref[first:first + n, :], i1_ref[first:first + n, :]
                vb, ib = v2_ref[fixed:fixed + 1, :], i2_ref[fixed:fixed + 1, :]
            cand_ref[r:r + n, :] = va + vb
            cidx_ref[r:r + n, :] = ia * PEER_KEYS + ib
            r += n
        cand = jnp.where(dup, -jnp.inf, cand_ref[...])
        _topk_rows(cand, flat, cidx_ref[...], sc_ref, eidx_ref, h * K)
        sc = sc_ref[...]
        e = jnp.exp(sc - sc[0:1, :])
        gate_ref[h * K:(h + 1) * K, :] = e / jnp.sum(e, axis=0, keepdims=True)


GT_LC = D_MODEL // 128


def _pack_kernel(u_ref, v_ref, o_ref):
    ub = pltpu.bitcast(u_ref[...].astype(bf16).astype(f32), jnp.uint32)
    vb = pltpu.bitcast(v_ref[...].astype(bf16).astype(f32), jnp.uint32)
    w = ub | (vb >> 16)
    tm = w.shape[0]
    for c in range(GT_LC):
        o_ref[pl.ds(c, tm, stride=GT_LC), :] = w[:, c * 128:(c + 1) * 128]


def _pack_tables(u_emb, v_emb):
    n = u_emb.shape[0]
    tm = 256
    spec = pl.BlockSpec((tm, D_MODEL), lambda i: (i, 0))
    return pl.pallas_call(
        _pack_kernel,
        out_shape=jax.ShapeDtypeStruct((n * GT_LC, 128), jnp.uint32),
        grid=(n // tm,),
        in_specs=[spec, spec],
        out_specs=pl.BlockSpec((tm * GT_LC, 128), lambda i: (i, 0)),
        compiler_params=_cparams(("parallel",)),
        name="peer_pack",
    )(u_emb, v_emb)


GT_TQ = 8
GT_ROWS = GT_TQ * PEER_SLOTS
GT_SUB = 8
GT_RG = PEER_SLOTS // GT_SUB


FS_NBUF = 4
FS_LOOK = FS_NBUF - 1
FS_PITCH = GT_LC + 1
FS_VMEM_LIMIT = 62 * 1024 * 1024


def _fused_kernel(ftok, hsc_ref, hsn_ref, gn_ref, gf_ref, wq_ref, k1_ref, k2_ref, tbl_ref, o_ref,
                  buf0_ref, buf1_ref, buf2_ref, buf3_ref, gsem_ref, eidx_ref, eidt_ref, gate_ref, idx_ref, csem_ref, gsub_ref,
                  v1_ref, i1_ref, v2_ref, i2_ref, cand_ref, cidx_ref, sc_ref, acc_ref):
    nsub = ftok // GT_TQ
    bufs = (buf0_ref, buf1_ref, buf2_ref, buf3_ref)
    i = pl.program_id(0)
    nt = pl.num_programs(0)

    def route(hs, slot):
        _route_tile(hs, gn_ref, wq_ref, k1_ref, k2_ref, eidx_ref, gate_ref,
                    v1_ref, i1_ref, v2_ref, i2_ref, cand_ref, cidx_ref, sc_ref)
        eidt_ref[...] = eidx_ref[...].T * GT_LC
        cp = pltpu.make_async_copy(eidt_ref, idx_ref.at[slot], csem_ref)
        cp.start()
        gate = gate_ref[...]
        for sub in range(nsub):
            gsub_ref[slot, sub, :, 0:GT_TQ] = gate[:, sub * GT_TQ:(sub + 1) * GT_TQ]
        cp.wait()

    def issue_token(slot, sub, tt, b):
        for k in range(PEER_SLOTS):
            line = pl.multiple_of(idx_ref[slot, sub * GT_TQ + tt, k], GT_LC)
            pltpu.make_async_copy(tbl_ref.at[pl.ds(line, GT_LC), :],
                                  bufs[b].at[pl.ds((tt * PEER_SLOTS + k) * FS_PITCH, GT_LC), :],
                                  gsem_ref.at[b]).start(priority=k % 2)

    def wait_buf(b):
        span = bufs[b].at[pl.ds(0, GT_ROWS * GT_LC), :]
        pltpu.make_async_copy(span, span, gsem_ref.at[b]).wait()

    @pl.when(i == 0)
    def _():
        route(hsc_ref[...], 0)
        for g in range(FS_LOOK):
            for tt in range(GT_TQ):
                issue_token(0, g, tt, g)

    @pl.when(i + 1 < nt)
    def _():
        route(hsn_ref[...], (i + 1) % 2)

    hi_mask = jnp.uint32(0xFFFF0000)

    def tile(buf_ref, t, g, c):
        return buf_ref[pl.ds((t * GT_RG + g) * GT_SUB * FS_PITCH + c, GT_SUB, stride=FS_PITCH), :]

    def quad(jj, carry):
        for u in range(FS_NBUF):
            sub = jj * FS_NBUF + u
            ahead = sub + FS_LOOK
            over = jnp.where(ahead >= nsub, 1, 0)
            aslot = (i + over) % 2
            asub = ahead - over * nsub
            ab = (u + FS_LOOK) % FS_NBUF
            wait_buf(u)
            cur_ref = bufs[u]
            row0 = pl.multiple_of(sub * GT_TQ, GT_TQ)
            hs = hsc_ref[pl.ds(row0, GT_TQ), :]
            h2b = _rms(hs, gn_ref[...]).astype(bf16).astype(f32)
            gate = gsub_ref[i % 2, sub]
            for t in range(GT_TQ):
                issue_token(aslot, asub, t, ab)
                um = jnp.concatenate(
                    [jnp.concatenate([pltpu.bitcast(tile(cur_ref, t, g, c) & hi_mask, f32) for c in range(GT_LC)], axis=1)
                     for g in range(GT_RG)], axis=0)
                a = _dot_nt(um, h2b)[:, t:t + 1]
                w = jnp.broadcast_to(gate[:, t:t + 1] * jax.nn.gelu(a), (PEER_SLOTS, 128))
                for c in range(GT_LC):
                    o = None
                    for g in range(GT_RG):
                        v = pltpu.bitcast(tile(cur_ref, t, g, c) << 16, f32) * w[g * GT_SUB:(g + 1) * GT_SUB]
                        o = v if o is None else o + v
                    acc_ref[t:t + 1, c * 128:(c + 1) * 128] = jnp.sum(o, axis=0, keepdims=True)
            o_ref[pl.ds(row0, GT_TQ), :] = _rms(hs + acc_ref[...], gf_ref[...])
        return carry
    lax.fori_loop(0, nsub // FS_NBUF, quad, 0)

    @pl.when(i == nt - 1)
    def _():
        for b in range(FS_LOOK):
            wait_buf(b)


def _peer_fused(hs2, norm_ffn, norm_g, w_q, k1, k2, tbl, ftok=128):
    rows = hs2.shape[0]
    nt = rows // ftok
    nsub = ftok // GT_TQ
    assert nsub % FS_NBUF == 0 and nsub > FS_LOOK
    K = PEER_TOPK
    buf = pltpu.VMEM((GT_ROWS * FS_PITCH, 128), jnp.uint32)
    return pl.pallas_call(
        functools.partial(_fused_kernel, ftok),
        out_shape=jax.ShapeDtypeStruct((rows, D_MODEL), f32),
        grid=(nt,),
        in_specs=[
            pl.BlockSpec((ftok, D_MODEL), lambda i: (i, 0)),
            pl.BlockSpec((ftok, D_MODEL), lambda i: (jnp.minimum(i + 1, nt - 1), 0)),
            _resident((1, D_MODEL)),
            _resident((1, D_MODEL)),
            _resident((D_MODEL, PEER_HEADS * 2 * PEER_HALF)),
            _resident((PEER_HEADS, PEER_KEYS, PEER_HALF)),
            _resident((PEER_HEADS, PEER_KEYS, PEER_HALF)),
            pl.BlockSpec(memory_space=pl.ANY),
        ],
        out_specs=pl.BlockSpec((ftok, D_MODEL), lambda i: (i, 0)),
        scratch_shapes=[
            buf, buf, buf, buf,
            pltpu.SemaphoreType.DMA((FS_NBUF,)),
            pltpu.VMEM((PEER_SLOTS, ftok), jnp.int32),
            pltpu.VMEM((ftok, PEER_SLOTS), jnp.int32),
            pltpu.VMEM((PEER_SLOTS, ftok), f32),
            pltpu.SMEM((2, ftok, PEER_SLOTS), jnp.int32),
            pltpu.SemaphoreType.DMA(()),
            pltpu.VMEM((2, nsub, PEER_SLOTS, 128), f32),
            pltpu.VMEM((K, ftok), f32), pltpu.VMEM((K, ftok), jnp.int32),
            pltpu.VMEM((K, ftok), f32), pltpu.VMEM((K, ftok), jnp.int32),
            pltpu.VMEM((_CAND_ROWS, ftok), f32), pltpu.VMEM((_CAND_ROWS, ftok), jnp.int32),
            pltpu.VMEM((K, ftok), f32),
            pltpu.VMEM((GT_TQ, D_MODEL), f32),
        ],
        compiler_params=_cparams(("arbitrary",), FS_VMEM_LIMIT),
        name="peer_fused",
    )(hs2, hs2, norm_ffn.astype(f32)[None], norm_g.astype(f32)[None], w_q.astype(bf16), k1.astype(bf16), k2.astype(bf16), tbl)


def kernel(x, meta_tokens, norm_mix, w_in, b_gate, ssm_lam_re, ssm_lam_im, ssm_log_dt, ssm_b_re, ssm_b_im, ssm_c_re, ssm_c_im, ssm_d, ssm_w_glu, ssm_b_glu, attn_sink, w_proj_ssm, w_proj_attn, w_out, norm_ffn, peer_w_q, peer_k1, peer_k2, peer_u, peer_v, norm_final):
    nb, seq, d = x.shape
    assert w_in.shape[0] == 1 and d == D_MODEL and seq % GI_TM == 0
    l = 0
    lp = seq + PAD_FRONT
    nc = lp // Q

    front = jnp.concatenate([jnp.zeros((PAD_FRONT - N_META, d), x.dtype), meta_tokens.astype(x.dtype)], axis=0)
    ut, qs, kvs = _mixin(x, front, norm_mix[l], w_in[l])
    gates = _gatein(x.reshape(nb * seq, d), norm_mix[l], w_in[l], b_gate[l])

    kfull, wb, wc, coef = _s5_weights(ssm_lam_re[l], ssm_lam_im[l], ssm_log_dt[l], ssm_b_re[l], ssm_b_im[l],
                                      ssm_c_re[l], ssm_c_im[l])
    yt = _s5(ut, nb, nc, kfull, wb, wc, coef, ssm_d[l])

    yb = _attention(qs, kvs, attn_sink[l], nb, seq // BLOCK)

    hs2 = _merge(yt, yb, gates, x, ssm_w_glu[l], ssm_b_glu[l], w_proj_ssm[l], w_proj_attn[l], w_out[l])
    hs2 = hs2.reshape(nb * seq, d)

    tbl = _pack_tables(peer_u[l], peer_v[l])
    out = _peer_fused(hs2, norm_ffn[l], norm_final, peer_w_q[l], peer_k1[l], peer_k2[l], tbl)
    return out.reshape(nb, seq, d)
```

```python
import functools
import math

import numpy as np
import jax
import jax.numpy as jnp
from jax import lax
from jax.experimental import pallas as pl
from jax.experimental.pallas import tpu as pltpu

f32 = jnp.float32
bf16 = jnp.bfloat16

D_MODEL = 2048
N_META = 16
SSM_WIDTH = 1024
SSM_GROUP = 16
SSM_GROUPS = 64
SSM_STATE = 64
HEAD_DIM = 128
N_HEADS = 8
N_KV_HEADS = 2
ATTN_WIDTH = 1024
KV_WIDTH = 256
WINDOW = 128
BLOCK = 128
PEER_HEADS = 8
PEER_KEYS = 128
PEER_HALF = 128
PEER_TOPK = 16
PEER_SLOTS = PEER_HEADS * PEER_TOPK
NORM_EPS = 1e-6
NEG_INF = -1e30

PAD_FRONT = 256
VMEM_LIMIT = 56 * 1024 * 1024

HI = lax.Precision.HIGHEST


def _cparams(sem, vmem=None):
    return pltpu.CompilerParams(dimension_semantics=sem, vmem_limit_bytes=vmem)


def _rms(x, g):
    ms = jnp.mean(x * x, axis=-1, keepdims=True)
    return x * lax.rsqrt(ms + NORM_EPS) * g


def _dot_nt(a, b):
    return lax.dot_general(a, b, (((1,), (1,)), ((), ())), preferred_element_type=f32)


def _resident(shape):
    return pl.BlockSpec(shape, lambda *_: (0,) * len(shape), pipeline_mode=pl.Buffered(1))


MX_TM = PAD_FRONT


def _mixin_kernel(x_ref, front_ref, g_ref, wut_ref, wqkv_ref, ut_ref, q_ref, kv_ref):
    xin = jnp.where(pl.program_id(1) == 0, front_ref[...], x_ref[0])
    xn = _rms(xin, g_ref[...]).astype(bf16)
    ut_ref[...] = _dot_nt(wut_ref[...], xn).astype(bf16)
    z = jnp.dot(xn, wqkv_ref[...], preferred_element_type=f32)
    q_ref[...] = (z[:, :ATTN_WIDTH] * (HEAD_DIM ** -0.5)).astype(bf16)
    kv_ref[...] = z[:, ATTN_WIDTH:].astype(bf16)


def _mixin(x, front, norm_g, w_in):
    nb, seq, d = x.shape
    nt = seq // MX_TM + 1
    rows = nb * nt * MX_TM
    wut = w_in[:, :SSM_WIDTH].T.astype(bf16)
    wqkv = w_in[:, SSM_WIDTH:SSM_WIDTH + ATTN_WIDTH + 2 * KV_WIDTH].astype(bf16)
    return pl.pallas_call(
        _mixin_kernel,
        out_shape=(
            jax.ShapeDtypeStruct((SSM_WIDTH, rows), bf16),
            jax.ShapeDtypeStruct((rows, ATTN_WIDTH), bf16),
            jax.ShapeDtypeStruct((rows, 2 * KV_WIDTH), bf16),
        ),
        grid=(nb, nt),
        in_specs=[
            pl.BlockSpec((1, MX_TM, d), lambda b, i: (b, jnp.maximum(i - 1, 0), 0)),
            _resident((MX_TM, d)),
            _resident((1, d)),
            _resident((SSM_WIDTH, d)),
            _resident((d, ATTN_WIDTH + 2 * KV_WIDTH)),
        ],
        out_specs=(
            pl.BlockSpec((SSM_WIDTH, MX_TM), lambda b, i: (0, b * nt + i)),
            pl.BlockSpec((MX_TM, ATTN_WIDTH), lambda b, i: (b * nt + i, 0)),
            pl.BlockSpec((MX_TM, 2 * KV_WIDTH), lambda b, i: (b * nt + i, 0)),
        ),
        compiler_params=_cparams(("parallel", "parallel"), VMEM_LIMIT),
        name="mixer_in",
    )(x, front, norm_g.astype(f32)[None], wut, wqkv)


GI_TM = 512
GI_TN = 4096


def _gatein_kernel(x_ref, g_ref, w_ref, b_ref, o_ref, xn_ref):
    @pl.when(pl.program_id(1) == 0)
    def _():
        xn_ref[...] = _rms(x_ref[...], g_ref[...]).astype(bf16)

    z = jnp.dot(xn_ref[...], w_ref[...], preferred_element_type=f32)
    o_ref[...] = jax.nn.sigmoid(z + b_ref[...]).astype(bf16)


def _gatein(x2d, norm_g, w_in, b_gate):
    rows, d = x2d.shape
    wg = w_in[:, SSM_WIDTH + ATTN_WIDTH + 2 * KV_WIDTH:].astype(bf16)
    ng = wg.shape[1]
    return pl.pallas_call(
        _gatein_kernel,
        out_shape=jax.ShapeDtypeStruct((rows, ng), bf16),
        grid=(rows // GI_TM, ng // GI_TN),
        in_specs=[
            pl.BlockSpec((GI_TM, d), lambda i, j: (i, 0)),
            _resident((1, d)),
            pl.BlockSpec((d, GI_TN), lambda i, j: (0, j)),
            pl.BlockSpec((1, GI_TN), lambda i, j: (0, j)),
        ],
        out_specs=pl.BlockSpec((GI_TM, GI_TN), lambda i, j: (i, j)),
        scratch_shapes=[pltpu.VMEM((GI_TM, d), bf16)],
        compiler_params=_cparams(("parallel", "arbitrary"), VMEM_LIMIT),
        name="gate_in",
    )(x2d, norm_g.astype(f32)[None], wg, b_gate.astype(f32)[None])


Q = BLOCK
GW = SSM_GROUP * Q


def _s5_weights(lam_re, lam_im, log_dt, b_re, b_im, c_re, c_im):
    dt = jnp.exp(log_dt.astype(f32))[..., None]
    lr = lam_re.astype(f32)
    li = lam_im.astype(f32)
    k = jnp.arange(Q + 1, dtype=f32)
    mag = jnp.exp(lr[..., None] * dt[..., None] * k)
    ang = li[..., None] * dt[..., None] * k
    ar = mag * jnp.cos(ang)
    ai = mag * jnp.sin(ang)
    a1r, a1i = ar[..., 1], ai[..., 1]
    den = lr * lr + li * li
    nr, ni = a1r - 1.0, a1i
    cr = (nr * lr + ni * li) / den
    ci = (ni * lr - nr * li) / den
    br, bi = b_re.astype(f32), b_im.astype(f32)
    bbr = cr[..., None] * br - ci[..., None] * bi
    bbi = cr[..., None] * bi + ci[..., None] * br
    ccr = jnp.swapaxes(c_re.astype(f32), -1, -2)
    cci = jnp.swapaxes(c_im.astype(f32), -1, -2)

    cbr = ccr[..., :, None] * bbr[..., None, :] - cci[..., :, None] * bbi[..., None, :]
    cbi = ccr[..., :, None] * bbi[..., None, :] + cci[..., :, None] * bbr[..., None, :]
    G, P, C = cbr.shape[1], cbr.shape[2], cbr.shape[3]
    cbr2 = cbr.reshape(2, G, P, C * C)
    cbi2 = cbi.reshape(2, G, P, C * C)
    kk = (jnp.einsum('dgpk,dgpx->dgkx', ar[..., :Q], cbr2, precision=HI)
          - jnp.einsum('dgpk,dgpx->dgkx', ai[..., :Q], cbi2, precision=HI))
    kk = kk.reshape(2, G, Q, C, C)
    kf, kb = kk[0], kk[1]
    lag0 = (kf[:, 0] + kb[:, 0])[:, None]
    zero = jnp.zeros_like(lag0)
    kfull = jnp.concatenate([lag0, kf[:, 1:], zero, kb[:, 1:][:, ::-1]], axis=1)
    kfull = jnp.transpose(kfull, (0, 3, 2, 1))

    def bu_pow(d, pw_r, pw_i):
        pr = jnp.swapaxes(pw_r, 1, 2)[:, None]
        pi = jnp.swapaxes(pw_i, 1, 2)[:, None]
        br_ = jnp.swapaxes(bbr[d], 1, 2)[:, :, None]
        bi_ = jnp.swapaxes(bbi[d], 1, 2)[:, :, None]
        return pr * br_ - pi * bi_, pr * bi_ + pi * br_
    fr, fi = bu_pow(0, ar[0][..., :Q][..., ::-1], ai[0][..., :Q][..., ::-1])
    rr, ri = bu_pow(1, ar[1][..., :Q], ai[1][..., :Q])
    wb = jnp.concatenate([fr, rr, fi, ri], axis=-1).reshape(G, C * Q, 4 * P)

    def c_pow(d, pw_r, pw_i):
        re = ccr[d][..., None] * pw_r[:, :, None, :] - cci[d][..., None] * pw_i[:, :, None, :]
        im = ccr[d][..., None] * pw_i[:, :, None, :] + cci[d][..., None] * pw_r[:, :, None, :]
        return re, -im
    f_re, f_im = c_pow(0, ar[0][..., 1:], ai[0][..., 1:])
    b_re_, b_im_ = c_pow(1, ar[1][..., 1:][..., ::-1], ai[1][..., 1:][..., ::-1])
    wc = jnp.concatenate([f_re, b_re_, f_im, b_im_], axis=1).reshape(G, 4 * P, C * Q)

    aqr, aqi = ar[..., Q], ai[..., Q]
    coef = jnp.stack([jnp.concatenate([aqr[0], aqr[1]], axis=-1),
                      jnp.concatenate([aqi[0], aqi[1]], axis=-1)], axis=1)
    coef = jnp.concatenate([coef, jnp.zeros((G, 6, 2 * P), f32)], axis=1)
    return kfull, wb.astype(bf16), wc.astype(bf16), coef


def _s5_kernel(nb, nc, ut_ref, kf_ref, wb_ref, wc_ref, coef_ref, dsk_ref, o_ref,
               x_ref, t_ref, cst_ref, hin_ref):
    C = SSM_GROUP
    P2 = 2 * SSM_STATE
    for c in range(C):
        x_ref[:, c * Q:(c + 1) * Q] = ut_ref[c]
    x = x_ref[...]

    def build(cp, carry):
        r0 = pl.multiple_of(cp * Q, Q)
        for c in range(C):
            row = kf_ref[0, cp, pl.ds(c, 1), :]
            blk = jnp.broadcast_to(row, (Q, 2 * Q))
            rolled = pltpu.roll(blk, 0, 1, stride=1, stride_axis=0)
            t_ref[pl.ds(r0, Q), c * Q:(c + 1) * Q] = rolled[:, :Q].astype(bf16)
        return carry
    lax.fori_loop(0, C, build, 0)

    cst_ref[...] = jnp.dot(x, wb_ref[0], preferred_element_type=f32)
    P = SSM_STATE
    aqr = coef_ref[0, 0:1, :]
    aqi = coef_ref[0, 1:2, :]
    is_fwd = lax.broadcasted_iota(jnp.int32, (1, P2), 1) < P
    for b in range(nb):
        hr = jnp.zeros((1, P2), f32)
        hi = jnp.zeros((1, P2), f32)
        for n in range(nc):
            rf = b * nc + n
            rb = b * nc + (nc - 1 - n)
            hin_ref[rf:rf + 1, 0:P] = hr[:, 0:P]
            hin_ref[rb:rb + 1, P:P2] = hr[:, P:P2]
            hin_ref[rf:rf + 1, P2:P2 + P] = hi[:, 0:P]
            hin_ref[rb:rb + 1, P2 + P:2 * P2] = hi[:, P:P2]
            cr = jnp.where(is_fwd, cst_ref[rf:rf + 1, 0:P2], cst_ref[rb:rb + 1, 0:P2])
            ci = jnp.where(is_fwd, cst_ref[rf:rf + 1, P2:2 * P2], cst_ref[rb:rb + 1, P2:2 * P2])
            hr, hi = aqr * hr - aqi * hi + cr, aqr * hi + aqi * hr + ci

    hin = hin_ref[...].astype(bf16)
    for cp in range(C // 2):
        cols = slice(cp * 2 * Q, (cp + 1) * 2 * Q)
        y2 = (jnp.dot(x, t_ref[:, cols], preferred_element_type=f32)
              + jnp.dot(hin, wc_ref[0, :, cols], preferred_element_type=f32))
        for h in range(2):
            c = 2 * cp + h
            y = y2[:, h * Q:(h + 1) * Q] + dsk_ref[0, c:c + 1, :] * ut_ref[c].astype(f32)
            o_ref[c] = jax.nn.gelu(y)


def _s5(ut, nb, nc, kfull, wb, wc, coef, d_skip):
    rows = nb * nc
    ut3 = ut.reshape(SSM_WIDTH, rows, Q)
    dsk = jnp.broadcast_to(d_skip.astype(f32).reshape(SSM_GROUPS, SSM_GROUP, 1), (SSM_GROUPS, SSM_GROUP, Q))
    C = SSM_GROUP
    out = pl.pallas_call(
        functools.partial(_s5_kernel, nb, nc),
        out_shape=jax.ShapeDtypeStruct((SSM_WIDTH, rows, Q), f32),
        grid=(SSM_GROUPS,),
        in_specs=[
            pl.BlockSpec((C, rows, Q), lambda g: (g, 0, 0)),
            pl.BlockSpec((1, C, C, 2 * Q), lambda g: (g, 0, 0, 0)),
            pl.BlockSpec((1, GW, 4 * SSM_STATE), lambda g: (g, 0, 0)),
            pl.BlockSpec((1, 4 * SSM_STATE, GW), lambda g: (g, 0, 0)),
            pl.BlockSpec((1, 8, 2 * SSM_STATE), lambda g: (g, 0, 0)),
            pl.BlockSpec((1, C, Q), lambda g: (g, 0, 0)),
        ],
        out_specs=pl.BlockSpec((C, rows, Q), lambda g: (g, 0, 0)),
        scratch_shapes=[
            pltpu.VMEM((rows, GW), bf16),
            pltpu.VMEM((GW, GW), bf16),
            pltpu.VMEM((rows, 4 * SSM_STATE), f32),
            pltpu.VMEM((rows, 4 * SSM_STATE), f32),
        ],
        compiler_params=_cparams(("parallel",), VMEM_LIMIT),
        name="s5_mixer",
    )(ut3, kfull, wb, wc, coef, dsk)
    return out.reshape(SSM_WIDTH, rows * Q)


def _alibi_slopes():
    return [float(2.0 ** (-8.0 * (h + 1) / N_HEADS)) for h in range(N_HEADS)]


def _attn_kernel(q_ref, km_ref, kp_ref, ko_ref, kn_ref, sink_ref, o_ref):
    jq = pl.program_id(1)
    nq = pl.num_programs(1)
    kv = jnp.concatenate([km_ref[...], kp_ref[...], ko_ref[...], kn_ref[...]], axis=0)
    nk = 4 * BLOCK
    row = lax.broadcasted_iota(jnp.int32, (BLOCK, nk), 0)
    col = lax.broadcasted_iota(jnp.int32, (BLOCK, nk), 1)
    seg = col // BLOCK
    cc = col % BLOCK
    rel = (seg - 2) * BLOCK + cc - row
    dist = jnp.abs(rel)
    band = (seg >= 1) & (dist <= WINDOW)
    band = band & jnp.logical_not((seg == 1) & (jq == 0))
    band = band & jnp.logical_not((seg == 3) & (jq == nq - 1))
    meta = (seg == 0) & (cc >= BLOCK - N_META)
    distf = dist.astype(f32)
    slopes = _alibi_slopes()
    grp = N_HEADS // N_KV_HEADS
    for h in range(N_HEADS):
        g = h // grp
        qh = q_ref[:, h * HEAD_DIM:(h + 1) * HEAD_DIM]
        kh = kv[:, g * HEAD_DIM:(g + 1) * HEAD_DIM]
        vh = kv[:, KV_WIDTH + g * HEAD_DIM:KV_WIDTH + (g + 1) * HEAD_DIM]
        s = _dot_nt(qh, kh)
        s = jnp.where(band, s - slopes[h] * distf, jnp.where(meta, s, NEG_INF))
        sink = sink_ref[h]
        m = jnp.maximum(jnp.max(s, axis=1, keepdims=True), sink)
        p = jnp.exp(s - m)
        den = jnp.sum(p, axis=1, keepdims=True) + jnp.exp(sink - m)
        o = jnp.dot(p.astype(bf16), vh, preferred_element_type=f32) / den
        o_ref[:, h * HEAD_DIM:(h + 1) * HEAD_DIM] = o.astype(bf16)


def _attention(qs, kvs, sink, nb, nq):
    pb = PAD_FRONT // BLOCK
    nbp = nq + pb
    kv_spec = lambda f: pl.BlockSpec((BLOCK, 2 * KV_WIDTH), f)
    return pl.pallas_call(
        _attn_kernel,
        out_shape=jax.ShapeDtypeStruct((nb * nq * BLOCK, ATTN_WIDTH), bf16),
        grid=(nb, nq),
        in_specs=[
            pl.BlockSpec((BLOCK, ATTN_WIDTH), lambda b, j: (b * nbp + pb + j, 0)),
            kv_spec(lambda b, j: (b * nbp + pb - 1, 0)),
            kv_spec(lambda b, j: (b * nbp + pb + j - 1, 0)),
            kv_spec(lambda b, j: (b * nbp + pb + j, 0)),
            kv_spec(lambda b, j: (b * nbp + jnp.minimum(pb + j + 1, nbp - 1), 0)),
            pl.BlockSpec(memory_space=pltpu.SMEM),
        ],
        out_specs=pl.BlockSpec((BLOCK, ATTN_WIDTH), lambda b, j: (b * nq + j, 0)),
        compiler_params=_cparams(("parallel", "parallel")),
        name="window_attn",
    )(qs, kvs, kvs, kvs, kvs, sink.astype(f32))


MG_TM = 256


def _merge_kernel(yt_ref, yb_ref, g0_ref, g1_ref, x_ref, wglu_ref, bglu_ref, wps_ref, wpa_ref, wout_ref, o_ref):
    y = yt_ref[...].T
    gl = jnp.dot(y.astype(bf16), wglu_ref[...], preferred_element_type=f32) + bglu_ref[...]
    ya = y * jax.nn.sigmoid(gl)
    pa = jnp.dot(ya.astype(bf16), wps_ref[...], preferred_element_type=f32)
    pbv = jnp.dot(yb_ref[...], wpa_ref[...], preferred_element_type=f32)
    merged = (g0_ref[...].astype(f32) * pa + g1_ref[...].astype(f32) * pbv).astype(bf16)
    o_ref[0] = x_ref[0] + jnp.dot(merged, wout_ref[...], preferred_element_type=f32)


def _merge(yt, yb, gates, x, w_glu, b_glu, w_ps, w_pa, w_out):
    nb, seq, d = x.shape
    tm = MG_TM
    nt = seq // tm
    lp_t = (seq + PAD_FRONT) // tm
    off = PAD_FRONT // tm
    return pl.pallas_call(
        _merge_kernel,
        out_shape=jax.ShapeDtypeStruct((nb, seq, d), f32),
        grid=(nb, nt),
        in_specs=[
            pl.BlockSpec((SSM_WIDTH, tm), lambda b, i: (0, b * lp_t + off + i)),
            pl.BlockSpec((tm, ATTN_WIDTH), lambda b, i: (b * nt + i, 0)),
            pl.BlockSpec((tm, d), lambda b, i: (b * nt + i, 0)),
            pl.BlockSpec((tm, d), lambda b, i: (b * nt + i, 1)),
            pl.BlockSpec((1, tm, d), lambda b, i: (b, i, 0)),
            _resident((SSM_WIDTH, SSM_WIDTH)),
            _resident((1, SSM_WIDTH)),
            _resident((SSM_WIDTH, d)),
            _resident((ATTN_WIDTH, d)),
            _resident((d, d)),
        ],
        out_specs=pl.BlockSpec((1, tm, d), lambda b, i: (b, i, 0)),
        compiler_params=_cparams(("parallel", "parallel"), VMEM_LIMIT),
        name="merge_out",
    )(yt, yb, gates, gates, x, w_glu.astype(bf16), b_glu.astype(f32)[None], w_ps.astype(bf16), w_pa.astype(bf16),
      w_out.astype(bf16))


def _topk_rows(s, order, payload, vals_ref, pay_ref, r0):
    big = jnp.int32(2 ** 30)
    for i in range(PEER_TOPK):
        m = jnp.max(s, axis=0, keepdims=True)
        idx = jnp.min(jnp.where(s == m, order, big), axis=0, keepdims=True)
        sel = order == idx
        vals_ref[i:i + 1, :] = m
        if payload is None:
            pay_ref[r0 + i:r0 + i + 1, :] = idx
        else:
            pay_ref[r0 + i:r0 + i + 1, :] = jnp.max(jnp.where(sel, payload, -1), axis=0, keepdims=True)
        s = jnp.where(sel, -jnp.inf, s)


_CAND_BLOCKS = (
    ("a", 0, 0, 16, 0), ("a", 1, 0, 8, 0), ("a", 2, 0, 8, 0), ("a", 3, 0, 8, 0),
    ("b", 0, 0, 8, 4), ("b", 1, 0, 8, 4), ("b", 2, 0, 8, 4), ("b", 0, 8, 8, 0),
)
_CAND_ROWS = sum(blk[3] for blk in _CAND_BLOCKS)


def _route_tile(hs, g_ref, wq_ref, k1_ref, k2_ref, eidx_ref, gate_ref,
                v1_ref, i1_ref, v2_ref, i2_ref, cand_ref, cidx_ref, sc_ref):
    h2 = _rms(hs, g_ref[...])
    qf = jnp.dot(h2.astype(bf16), wq_ref[...], preferred_element_type=f32).astype(bf16)
    K = PEER_TOPK
    tm = qf.shape[0]
    keyid = lax.broadcasted_iota(jnp.int32, (PEER_KEYS, tm), 0)
    flat, dup = [], []
    for axis, fixed, first, n, drop in _CAND_BLOCKS:
        var = lax.broadcasted_iota(jnp.int32, (n, tm), 0) + first
        flat.append(fixed * K + var if axis == "a" else var * K + fixed)
        dup.append(var < drop)
    flat = jnp.concatenate(flat, axis=0)
    dup = jnp.concatenate(dup, axis=0)
    for h in range(PEER_HEADS):
        q1 = qf[:, h * 2 * PEER_HALF:h * 2 * PEER_HALF + PEER_HALF]
        q2 = qf[:, h * 2 * PEER_HALF + PEER_HALF:(h + 1) * 2 * PEER_HALF]
        s1 = _dot_nt(k1_ref[h], q1)
        s2 = _dot_nt(k2_ref[h], q2)
        _topk_rows(s1, keyid, None, v1_ref, i1_ref, 0)
        _topk_rows(s2, keyid, None, v2_ref, i2_ref, 0)
        r = 0
        for axis, fixed, first, n, drop in _CAND_BLOCKS:
            if axis == "a":
                va, ia = v1_ref[fixed:fixed + 1, :], i1_ref[fixed:fixed + 1, :]
                vb, ib = v2_ref[first:first + n, :], i2_ref[first:first + n, :]
            else:
                va, ia = v1_ref[first:first + n, :], i1_ref[first:first + n, :]
                vb, ib = v2_ref[fixed:fixed + 1, :], i2_ref[fixed:fixed + 1, :]
            cand_ref[r:r + n, :] = va + vb
            cidx_ref[r:r + n, :] = ia * PEER_KEYS + ib
            r += n
        cand = jnp.where(dup, -jnp.inf, cand_ref[...])
        _topk_rows(cand, flat, cidx_ref[...], sc_ref, eidx_ref, h * K)
        sc = sc_ref[...]
        e = jnp.exp(sc - sc[0:1, :])
        gate_ref[h * K:(h + 1) * K, :] = e / jnp.sum(e, axis=0, keepdims=True)


GT_LC = D_MODEL // 128


def _pack_kernel(u_ref, v_ref, o_ref):
    ub = pltpu.bitcast(u_ref[...].astype(bf16).astype(f32), jnp.uint32)
    vb = pltpu.bitcast(v_ref[...].astype(bf16).astype(f32), jnp.uint32)
    w = ub | (vb >> 16)
    tm = w.shape[0]
    for c in range(GT_LC):
        o_ref[pl.ds(c, tm, stride=GT_LC), :] = w[:, c * 128:(c + 1) * 128]


def _pack_tables(u_emb, v_emb):
    n = u_emb.shape[0]
    tm = 256
    spec = pl.BlockSpec((tm, D_MODEL), lambda i: (i, 0))
    return pl.pallas_call(
        _pack_kernel,
        out_shape=jax.ShapeDtypeStruct((n * GT_LC, 128), jnp.uint32),
        grid=(n // tm,),
        in_specs=[spec, spec],
        out_specs=pl.BlockSpec((tm * GT_LC, 128), lambda i: (i, 0)),
        compiler_params=_cparams(("parallel",)),
        name="peer_pack",
    )(u_emb, v_emb)


GT_TQ = 8
GT_ROWS = GT_TQ * PEER_SLOTS
GT_SUB = 8
GT_RG = PEER_SLOTS // GT_SUB


FS_NBUF = 4
FS_LOOK = FS_NBUF - 1
FS_PITCH = GT_LC + 1
FS_VMEM_LIMIT = 62 * 1024 * 1024


def _fused_kernel(ftok, hsc_ref, hsn_ref, gn_ref, gf_ref, wq_ref, k1_ref, k2_ref, tbl_ref, o_ref,
                  buf0_ref, buf1_ref, buf2_ref, buf3_ref, gsem_ref, eidx_ref, eidt_ref, gate_ref, idx_ref, csem_ref, gsub_ref,
                  v1_ref, i1_ref, v2_ref, i2_ref, cand_ref, cidx_ref, sc_ref, acc_ref):
    nsub = ftok // GT_TQ
    bufs = (buf0_ref, buf1_ref, buf2_ref, buf3_ref)
    i = pl.program_id(0)
    nt = pl.num_programs(0)

    def route(hs, slot):
        _route_tile(hs, gn_ref, wq_ref, k1_ref, k2_ref, eidx_ref, gate_ref,
                    v1_ref, i1_ref, v2_ref, i2_ref, cand_ref, cidx_ref, sc_ref)
        eidt_ref[...] = eidx_ref[...].T * GT_LC
        cp = pltpu.make_async_copy(eidt_ref, idx_ref.at[slot], csem_ref)
        cp.start()
        gate = gate_ref[...]
        for sub in range(nsub):
            gsub_ref[slot, sub, :, 0:GT_TQ] = gate[:, sub * GT_TQ:(sub + 1) * GT_TQ]
        cp.wait()

    def issue_token(slot, sub, tt, b):
        for k in range(PEER_SLOTS):
            line = pl.multiple_of(idx_ref[slot, sub * GT_TQ + tt, k], GT_LC)
            pltpu.make_async_copy(tbl_ref.at[pl.ds(line, GT_LC), :],
                                  bufs[b].at[pl.ds((tt * PEER_SLOTS + k) * FS_PITCH, GT_LC), :],
                                  gsem_ref.at[b]).start(priority=k % 2)

    def wait_buf(b):
        span = bufs[b].at[pl.ds(0, GT_ROWS * GT_LC), :]
        pltpu.make_async_copy(span, span, gsem_ref.at[b]).wait()

    @pl.when(i == 0)
    def _():
        route(hsc_ref[...], 0)
        for g in range(FS_LOOK):
            for tt in range(GT_TQ):
                issue_token(0, g, tt, g)

    @pl.when(i + 1 < nt)
    def _():
        route(hsn_ref[...], (i + 1) % 2)

    hi_mask = jnp.uint32(0xFFFF0000)

    def tile(buf_ref, t, g, c):
        return buf_ref[pl.ds((t * GT_RG + g) * GT_SUB * FS_PITCH + c, GT_SUB, stride=FS_PITCH), :]

    def quad(jj, carry):
        for u in range(FS_NBUF):
            sub = jj * FS_NBUF + u
            ahead = sub + FS_LOOK
            over = jnp.where(ahead >= nsub, 1, 0)
            aslot = (i + over) % 2
            asub = ahead - over * nsub
            ab = (u + FS_LOOK) % FS_NBUF
            wait_buf(u)
            cur_ref = bufs[u]
            row0 = pl.multiple_of(sub * GT_TQ, GT_TQ)
            hs = hsc_ref[pl.ds(row0, GT_TQ), :]
            h2b = _rms(hs, gn_ref[...]).astype(bf16).astype(f32)
            gate = gsub_ref[i % 2, sub]
            for t in range(GT_TQ):
                issue_token(aslot, asub, t, ab)
                um = jnp.concatenate(
                    [jnp.concatenate([pltpu.bitcast(tile(cur_ref, t, g, c) & hi_mask, f32) for c in range(GT_LC)], axis=1)
                     for g in range(GT_RG)], axis=0)
                a = _dot_nt(um, h2b)[:, t:t + 1]
                w = jnp.broadcast_to(gate[:, t:t + 1] * jax.nn.gelu(a), (PEER_SLOTS, 128))
                for c in range(GT_LC):
                    o = None
                    for g in range(GT_RG):
                        v = pltpu.bitcast(tile(cur_ref, t, g, c) << 16, f32) * w[g * GT_SUB:(g + 1) * GT_SUB]
                        o = v if o is None else o + v
                    acc_ref[t:t + 1, c * 128:(c + 1) * 128] = jnp.sum(o, axis=0, keepdims=True)
            o_ref[pl.ds(row0, GT_TQ), :] = _rms(hs + acc_ref[...], gf_ref[...])
        return carry
    lax.fori_loop(0, nsub // FS_NBUF, quad, 0)

    @pl.when(i == nt - 1)
    def _():
        for b in range(FS_LOOK):
            wait_buf(b)


def _peer_fused(hs2, norm_ffn, norm_g, w_q, k1, k2, tbl, ftok=128):
    rows = hs2.shape[0]
    nt = rows // ftok
    nsub = ftok // GT_TQ
    assert nsub % FS_NBUF == 0 and nsub > FS_LOOK
    K = PEER_TOPK
    buf = pltpu.VMEM((GT_ROWS * FS_PITCH, 128), jnp.uint32)
    return pl.pallas_call(
        functools.partial(_fused_kernel, ftok),
        out_shape=jax.ShapeDtypeStruct((rows, D_MODEL), f32),
        grid=(nt,),
        in_specs=[
            pl.BlockSpec((ftok, D_MODEL), lambda i: (i, 0)),
            pl.BlockSpec((ftok, D_MODEL), lambda i: (jnp.minimum(i + 1, nt - 1), 0)),
            _resident((1, D_MODEL)),
            _resident((1, D_MODEL)),
            _resident((D_MODEL, PEER_HEADS * 2 * PEER_HALF)),
            _resident((PEER_HEADS, PEER_KEYS, PEER_HALF)),
            _resident((PEER_HEADS, PEER_KEYS, PEER_HALF)),
            pl.BlockSpec(memory_space=pl.ANY),
        ],
        out_specs=pl.BlockSpec((ftok, D_MODEL), lambda i: (i, 0)),
        scratch_shapes=[
            buf, buf, buf, buf,
            pltpu.SemaphoreType.DMA((FS_NBUF,)),
            pltpu.VMEM((PEER_SLOTS, ftok), jnp.int32),
            pltpu.VMEM((ftok, PEER_SLOTS), jnp.int32),
            pltpu.VMEM((PEER_SLOTS, ftok), f32),
            pltpu.SMEM((2, ftok, PEER_SLOTS), jnp.int32),
            pltpu.SemaphoreType.DMA(()),
            pltpu.VMEM((2, nsub, PEER_SLOTS, 128), f32),
            pltpu.VMEM((K, ftok), f32), pltpu.VMEM((K, ftok), jnp.int32),
            pltpu.VMEM((K, ftok), f32), pltpu.VMEM((K, ftok), jnp.int32),
            pltpu.VMEM((_CAND_ROWS, ftok), f32), pltpu.VMEM((_CAND_ROWS, ftok), jnp.int32),
            pltpu.VMEM((K, ftok), f32),
            pltpu.VMEM((GT_TQ, D_MODEL), f32),
        ],
        compiler_params=_cparams(("arbitrary",), FS_VMEM_LIMIT),
        name="peer_fused",
    )(hs2, hs2, norm_ffn.astype(f32)[None], norm_g.astype(f32)[None], w_q.astype(bf16), k1.astype(bf16), k2.astype(bf16), tbl)


def kernel(x, meta_tokens, norm_mix, w_in, b_gate, ssm_lam_re, ssm_lam_im, ssm_log_dt, ssm_b_re, ssm_b_im, ssm_c_re, ssm_c_im, ssm_d, ssm_w_glu, ssm_b_glu, attn_sink, w_proj_ssm, w_proj_attn, w_out, norm_ffn, peer_w_q, peer_k1, peer_k2, peer_u, peer_v, norm_final):
    nb, seq, d = x.shape
    assert w_in.shape[0] == 1 and d == D_MODEL and seq % GI_TM == 0
    l = 0
    lp = seq + PAD_FRONT
    nc = lp // Q

    front = jnp.concatenate([jnp.zeros((PAD_FRONT - N_META, d), x.dtype), meta_tokens.astype(x.dtype)], axis=0)
    ut, qs, kvs = _mixin(x, front, norm_mix[l], w_in[l])
    gates = _gatein(x.reshape(nb * seq, d), norm_mix[l], w_in[l], b_gate[l])

    kfull, wb, wc, coef = _s5_weights(ssm_lam_re[l], ssm_lam_im[l], ssm_log_dt[l], ssm_b_re[l], ssm_b_im[l],
                                      ssm_c_re[l], ssm_c_im[l])
    yt = _s5(ut, nb, nc, kfull, wb, wc, coef, ssm_d[l])

    yb = _attention(qs, kvs, attn_sink[l], nb, seq // BLOCK)

    hs2 = _merge(yt, yb, gates, x, ssm_w_glu[l], ssm_b_glu[l], w_proj_ssm[l], w_proj_attn[l], w_out[l])
    hs2 = hs2.reshape(nb * seq, d)

    tbl = _pack_tables(peer_u[l], peer_v[l])
    out = _peer_fused(hs2, norm_ffn[l], norm_final, peer_w_q[l], peer_k1[l], peer_k2[l], tbl)
    return out.reshape(nb, seq, d)
```
